```python
import jax, jax.numpy as jnp
from jax import lax
import numpy as np

D_MODEL = 1024
BATCH = 2
SEQ = 8192
DEPTH = 1
DEC_BATCH = 8
DEC_SEQ = 8192
PAST_LEN = 128

GRID_W = 64
POOL_WINDOWS = (2, 4, 8, 16)
POOL_GROUPS = 4
D_POOL = D_MODEL // 2
POOL_GROUP_C = D_POOL // POOL_GROUPS
POOL_OUT_C = D_MODEL // POOL_GROUPS
N_HEADS = D_MODEL // 128
D_ATT_QK = D_MODEL // 2
HEAD_DIM_QK = D_ATT_QK // N_HEADS
HEAD_DIM_V = D_MODEL // N_HEADS
NA_WIN_H = 8
NA_WIN_W = 16
D_IN = D_POOL + 2 * D_ATT_QK + D_MODEL + 2 * D_MODEL
D_FF = ((8 * D_MODEL + 3 * 256 - 1) // (3 * 256)) * 256
EPS = 1e-6

kernel_name = "hybrid_pool_natten_encoder"


def rmsnorm(x, g):
    x32 = x.astype(jnp.float32)
    y = x32 * lax.rsqrt(jnp.mean(x32 * x32, axis=-1, keepdims=True) + EPS)
    return (y * g.astype(jnp.float32)).astype(x.dtype)


def multiscale_pool(u, w_grp, scale):
    B, S, _ = u.shape
    u32 = u.astype(jnp.float32)
    cs = jnp.concatenate([jnp.zeros((B, 1, D_POOL), jnp.float32), jnp.cumsum(u32, axis=1)], axis=1)
    t = jnp.arange(S)
    outs = []
    for g, w in enumerate(POOL_WINDOWS):
        lo = jnp.clip(t - w // 2, 0, S)
        hi = jnp.clip(t + w // 2, 0, S)
        sl = slice(g * POOL_GROUP_C, (g + 1) * POOL_GROUP_C)
        csg = cs[:, :, sl]
        win_sum = jnp.take(csg, hi, axis=1) - jnp.take(csg, lo, axis=1)
        cnt = (hi - lo).astype(jnp.float32)[None, :, None]
        outs.append(win_sum / cnt - u32[:, :, sl])
    pooled = jnp.stack(outs, axis=2).astype(u.dtype)
    y = jnp.einsum('bsgc,gcd->bsgd', pooled, w_grp)
    return y.reshape(B, S, D_MODEL) * scale


def neighbourhood_attention(q, k, v, rpb):
    B, S = q.shape[0], q.shape[1]
    rows = S // GRID_W
    kh = min(NA_WIN_H, rows)
    kw = NA_WIN_W
    qg = q.reshape(B, rows, GRID_W, N_HEADS, HEAD_DIM_QK)
    kg = k.reshape(B, rows, GRID_W, N_HEADS, HEAD_DIM_QK)
    vg = v.reshape(B, rows, GRID_W, N_HEADS, HEAD_DIM_V)
    cols = jnp.arange(GRID_W)
    col_start = jnp.clip(cols - kw // 2, 0, GRID_W - kw)
    col_idx = col_start[:, None] + jnp.arange(kw)[None, :]
    dc = col_idx - cols[:, None]
    row_start = jnp.clip(jnp.arange(rows) - kh // 2, 0, rows - kh)
    rpb_c = rpb[:, :, dc + (NA_WIN_W - 1)]
    scale = HEAD_DIM_QK ** -0.5

    def one_row(r):
        rs = row_start[r]
        q_r = lax.dynamic_index_in_dim(qg, r, axis=1, keepdims=False)
        k_blk = lax.dynamic_slice_in_dim(kg, rs, kh, axis=1)
        v_blk = lax.dynamic_slice_in_dim(vg, rs, kh, axis=1)
        k_nb = k_blk[:, :, col_idx]
        v_nb = v_blk[:, :, col_idx]
        s = jnp.einsum('bchd,bkcjhd->bhckj', q_r, k_nb).astype(jnp.float32) * scale
        dr = rs + jnp.arange(kh) - r
        bias = rpb_c[:, dr + (NA_WIN_H - 1)]
        s = s + jnp.transpose(bias, (0, 2, 1, 3))[None].astype(jnp.float32)
        p = jax.nn.softmax(s.reshape(B, N_HEADS, GRID_W, kh * kw), axis=-1)
        p = p.reshape(B, N_HEADS, GRID_W, kh, kw).astype(v.dtype)
        return jnp.einsum('bhckj,bkcjhe->bche', p, v_nb)

    out = lax.map(one_row, jnp.arange(rows))
    return jnp.transpose(out, (1, 0, 2, 3, 4)).reshape(B, S, N_HEADS * HEAD_DIM_V)


def encoder_layer(x, norm_mix_pre, w_in, w_pool_grp, pool_scale, attn_rpb, w_out,
                  norm_mix_post, norm_ffn_pre, w_gate_up, w_down, norm_ffn_post):
    B, S, _ = x.shape
    h = rmsnorm(x, norm_mix_pre)
    z = h @ w_in
    o1 = D_POOL
    o2 = o1 + D_ATT_QK
    o3 = o2 + D_ATT_QK
    o4 = o3 + D_MODEL
    o5 = o4 + D_MODEL
    u_pool = z[..., :o1]
    q = z[..., o1:o2].reshape(B, S, N_HEADS, HEAD_DIM_QK)
    k = z[..., o2:o3].reshape(B, S, N_HEADS, HEAD_DIM_QK)
    v = z[..., o3:o4].reshape(B, S, N_HEADS, HEAD_DIM_V)
    g_pool = jax.nn.sigmoid(z[..., o4:o5])
    g_attn = jax.nn.sigmoid(z[..., o5:])
    a = multiscale_pool(u_pool, w_pool_grp, pool_scale)
    b = neighbourhood_attention(q, k, v, attn_rpb)
    m = g_pool * a + g_attn * b
    x = x + rmsnorm(m @ w_out, norm_mix_post)
    h = rmsnorm(x, norm_ffn_pre)
    gu = h @ w_gate_up
    f = (jax.nn.silu(gu[..., :D_FF]) * gu[..., D_FF:]) @ w_down
    return x + rmsnorm(f, norm_ffn_post)


def trunk(x, norm_mix_pre, w_in, w_pool_grp, pool_scale, attn_rpb, w_out,
          norm_mix_post, norm_ffn_pre, w_gate_up, w_down, norm_ffn_post):
    for l in range(DEPTH):
        x = encoder_layer(x, norm_mix_pre[l], w_in[l], w_pool_grp[l], pool_scale[l], attn_rpb[l],
                          w_out[l], norm_mix_post[l], norm_ffn_pre[l], w_gate_up[l], w_down[l],
                          norm_ffn_post[l])
    return x


def setup_inputs(seed: int = 0) -> dict:
    key = jax.random.key(seed)
    ks = jax.random.split(key, 14)
    f32 = jnp.float32

    def nrm(k, shape, s):
        return jax.random.normal(k, shape, f32) * s

    def gain(k):
        return 1.0 + nrm(k, (DEPTH, D_MODEL), 0.05)

    return {
        "x_prompt": nrm(ks[0], (BATCH, SEQ, D_MODEL), 1.0),
        "x_sample": nrm(ks[1], (DEC_BATCH, DEC_SEQ, D_MODEL), 1.0),
        "norm_mix_pre": gain(ks[2]),
        "w_in": nrm(ks[3], (DEPTH, D_MODEL, D_IN), D_MODEL ** -0.5),
        "w_pool_grp": nrm(ks[4], (DEPTH, POOL_GROUPS, POOL_GROUP_C, POOL_OUT_C), POOL_GROUP_C ** -0.5),
        "pool_scale": 1.0 + nrm(ks[5], (DEPTH, D_MODEL), 0.1),
        "attn_rpb": nrm(ks[6], (DEPTH, N_HEADS, 2 * NA_WIN_H - 1, 2 * NA_WIN_W - 1), 0.5),
        "w_out": nrm(ks[7], (DEPTH, D_MODEL, D_MODEL), D_MODEL ** -0.5),
        "norm_mix_post": gain(ks[8]),
        "norm_ffn_pre": gain(ks[9]),
        "w_gate_up": nrm(ks[10], (DEPTH, D_MODEL, 2 * D_FF), D_MODEL ** -0.5),
        "w_down": nrm(ks[11], (DEPTH, D_FF, D_MODEL), D_FF ** -0.5),
        "norm_ffn_post": gain(ks[12]),
    }


def reference(x_prompt, x_sample, norm_mix_pre, w_in, w_pool_grp, pool_scale, attn_rpb, w_out,
              norm_mix_post, norm_ffn_pre, w_gate_up, w_down, norm_ffn_post):
    y_prompt = trunk(x_prompt, norm_mix_pre, w_in, w_pool_grp, pool_scale, attn_rpb, w_out,
                     norm_mix_post, norm_ffn_pre, w_gate_up, w_down, norm_ffn_post)
    y_sample = trunk(x_sample, norm_mix_pre, w_in, w_pool_grp, pool_scale, attn_rpb, w_out,
                     norm_mix_post, norm_ffn_pre, w_gate_up, w_down, norm_ffn_post)
    return (y_prompt, y_sample)
```

```python
import functools

import jax
import jax.numpy as jnp
from jax import lax
from jax.experimental import pallas as pl
from jax.experimental.pallas import tpu as pltpu

D_MODEL = 1024
GRID_W = 64
POOL_WINDOWS = (2, 4, 8, 16)
POOL_GROUPS = 4
D_POOL = D_MODEL // 2
POOL_GROUP_C = D_POOL // POOL_GROUPS
POOL_OUT_C = D_MODEL // POOL_GROUPS
N_HEADS = D_MODEL // 128
D_ATT_QK = D_MODEL // 2
HEAD_DIM_QK = D_ATT_QK // N_HEADS
HEAD_DIM_V = D_MODEL // N_HEADS
NA_WIN_H = 8
NA_WIN_W = 16
D_IN = D_POOL + 2 * D_ATT_QK + D_MODEL + 2 * D_MODEL
D_FF = ((8 * D_MODEL + 3 * 256 - 1) // (3 * 256)) * 256
EPS = 1e-6

O_Q = D_POOL
O_K = O_Q + D_ATT_QK
O_V = O_K + D_ATT_QK
O_GP = O_V + D_MODEL
O_GA = O_GP + D_MODEL

QK_SCALE = HEAD_DIM_QK ** -0.5
MASK_VALUE = -1e30

TOKEN_TILE = 512
ROWS_PER_TILE = TOKEN_TILE // GRID_W
HALO_BEFORE = (NA_WIN_H // 2) * GRID_W
HALO_AFTER = HALO_BEFORE
KV_WINDOW = HALO_BEFORE + TOKEN_TILE + HALO_AFTER
WIN_TOKENS = NA_WIN_H * GRID_W
POOL_HALO = 8
FF_CHUNK = 256
VMEM_LIMIT_BYTES = 58 * 1024 * 1024


def _rmsnorm(x, g):
    y = x * lax.rsqrt(jnp.mean(x * x, axis=-1, keepdims=True) + EPS)
    return y * g


def _resident(shape):
    zeros = (0,) * len(shape)
    return pl.BlockSpec(shape, lambda b, i: zeros, pipeline_mode=pl.Buffered(1))


def _tile(width):
    return pl.BlockSpec((None, TOKEN_TILE, width), lambda b, i: (b, i, 0))


def _in_proj_kernel(x_ref, g_ref, w_ref, u_ref, q_ref, k_ref, v_ref, gp_ref, ga_ref):
    h = _rmsnorm(x_ref[...], g_ref[...]).astype(jnp.bfloat16)

    def proj(lo, width):
        return jnp.dot(h, w_ref[:, lo:lo + width], preferred_element_type=jnp.float32)

    u_ref[...] = proj(0, D_POOL)
    q_ref[...] = (proj(O_Q, D_ATT_QK) * QK_SCALE).astype(jnp.bfloat16)
    k_ref[...] = proj(O_K, D_ATT_QK).astype(jnp.bfloat16)
    for c in range(0, D_MODEL, 512):
        v_ref[:, c:c + 512] = proj(O_V + c, 512).astype(jnp.bfloat16)
        gp_ref[:, c:c + 512] = jax.nn.sigmoid(proj(O_GP + c, 512))
        ga_ref[:, c:c + 512] = jax.nn.sigmoid(proj(O_GA + c, 512))


def _in_proj(x, g_pre, w_in_bf16):
    batch, seq, _ = x.shape
    f32, bf16 = jnp.float32, jnp.bfloat16
    out_shape = (
        jax.ShapeDtypeStruct((batch, seq, D_POOL), f32),
        jax.ShapeDtypeStruct((batch, seq, D_ATT_QK), bf16),
        jax.ShapeDtypeStruct((batch, seq, D_ATT_QK), bf16),
        jax.ShapeDtypeStruct((batch, seq, D_MODEL), bf16),
        jax.ShapeDtypeStruct((batch, seq, D_MODEL), f32),
        jax.ShapeDtypeStruct((batch, seq, D_MODEL), f32),
    )
    return pl.pallas_call(
        _in_proj_kernel,
        grid=(batch, seq // TOKEN_TILE),
        in_specs=[_tile(D_MODEL), _resident((1, D_MODEL)), _resident((D_MODEL, D_IN))],
        out_specs=(_tile(D_POOL), _tile(D_ATT_QK), _tile(D_ATT_QK), _tile(D_MODEL),
                   _tile(D_MODEL), _tile(D_MODEL)),
        out_shape=out_shape,
        compiler_params=pltpu.CompilerParams(
            dimension_semantics=("arbitrary", "arbitrary"),
            vmem_limit_bytes=VMEM_LIMIT_BYTES),
        name="in_proj",
    )(x, g_pre, w_in_bf16)


def _mixer_kernel(x_ref, u_ref, u_prev_ref, u_next_ref, q_ref,
                  k_prev_ref, k_ref, k_next_ref, v_prev_ref, v_ref, v_next_ref,
                  gp_ref, ga_ref, bias_ref, w_pool_ref, pool_scale_ref, w_out_ref, g_post_ref,
                  o_ref, upad_ref, kwin_ref, vwin_ref, mix_ref, *, seq):
    i = pl.program_id(1)
    n_tiles = pl.num_programs(1)
    rows = seq // GRID_W

    upad_ref[0:POOL_HALO, :] = jnp.where(i > 0, u_prev_ref[...], 0.0)
    upad_ref[POOL_HALO:POOL_HALO + TOKEN_TILE, :] = u_ref[...]
    upad_ref[POOL_HALO + TOKEN_TILE:, :] = jnp.where(i < n_tiles - 1, u_next_ref[...], 0.0)

    t = i * TOKEN_TILE + lax.broadcasted_iota(jnp.int32, (TOKEN_TILE, POOL_GROUP_C), 0)
    for g, w in enumerate(POOL_WINDOWS):
        half = w // 2
        lanes = slice(g * POOL_GROUP_C, (g + 1) * POOL_GROUP_C)
        win_sum = upad_ref[POOL_HALO - half:POOL_HALO - half + TOKEN_TILE, lanes]
        for d in range(-half + 1, half):
            win_sum = win_sum + upad_ref[POOL_HALO + d:POOL_HALO + d + TOKEN_TILE, lanes]
        cnt = jnp.minimum(t + half, seq) - jnp.maximum(t - half, 0)
        pooled = win_sum / cnt.astype(jnp.float32) - u_ref[:, lanes]
        y = jnp.dot(pooled.astype(jnp.bfloat16), w_pool_ref[g],
                    preferred_element_type=jnp.float32)
        cols = slice(g * POOL_OUT_C, (g + 1) * POOL_OUT_C)
        mix_ref[:, cols] = gp_ref[:, cols] * (y * pool_scale_ref[:, cols])

    kwin_ref[0:HALO_BEFORE, :] = k_prev_ref[...]
    kwin_ref[HALO_BEFORE:HALO_BEFORE + TOKEN_TILE, :] = k_ref[...]
    kwin_ref[HALO_BEFORE + TOKEN_TILE:, :] = k_next_ref[...]
    vwin_ref[0:HALO_BEFORE, :] = v_prev_ref[...]
    vwin_ref[HALO_BEFORE:HALO_BEFORE + TOKEN_TILE, :] = v_ref[...]
    vwin_ref[HALO_BEFORE + TOKEN_TILE:, :] = v_next_ref[...]

    lane = lax.broadcasted_iota(jnp.int32, (GRID_W, 128), 1)
    first_row_in_window = i * ROWS_PER_TILE - NA_WIN_H // 2

    def row_body(rr, carry):
        r = i * ROWS_PER_TILE + rr
        row_start = jnp.clip(r - NA_WIN_H // 2, 0, rows - NA_WIN_H)
        delta = r - row_start
        start = pl.multiple_of((row_start - first_row_in_window) * GRID_W, GRID_W)
        tok = pl.multiple_of(rr * GRID_W, GRID_W)
        q_row = q_ref[pl.ds(tok, GRID_W), :]
        for h in range(N_HEADS):
            pair = slice((h // 2) * 128, (h // 2 + 1) * 128)
            in_head = (lane >= (h % 2) * HEAD_DIM_QK) & (lane < (h % 2 + 1) * HEAD_DIM_QK)
            q_h = jnp.where(in_head, q_row[:, pair], jnp.zeros((), jnp.bfloat16))
            k_h = kwin_ref[pl.ds(start, WIN_TOKENS), pair]
            s = lax.dot_general(q_h, k_h, (((1,), (1,)), ((), ())),
                                preferred_element_type=jnp.float32)
            s = s + bias_ref[delta, h]
            e = jnp.exp(s - jnp.max(s, axis=-1, keepdims=True))
            denom = jnp.sum(e, axis=-1, keepdims=True)
            vcols = slice(h * HEAD_DIM_V, (h + 1) * HEAD_DIM_V)
            o = jnp.dot(e.astype(jnp.bfloat16), vwin_ref[pl.ds(start, WIN_TOKENS), vcols],
                        preferred_element_type=jnp.float32)
            rows_out = pl.ds(tok, GRID_W)
            mix_ref[rows_out, vcols] = mix_ref[rows_out, vcols] + ga_ref[rows_out, vcols] * (o / denom)
        return carry

    lax.fori_loop(0, ROWS_PER_TILE, row_body, 0)

    y = jnp.dot(mix_ref[...].astype(jnp.bfloat16), w_out_ref[...],
                preferred_element_type=jnp.float32)
    o_ref[...] = x_ref[...] + _rmsnorm(y, g_post_ref[...])


def _bias_table(rpb):
    c = jnp.arange(GRID_W)
    col_start = jnp.clip(c - NA_WIN_W // 2, 0, GRID_W - NA_WIN_W)
    kc = jnp.arange(GRID_W)
    valid = (kc[None, :] >= col_start[:, None]) & (kc[None, :] < col_start[:, None] + NA_WIN_W)
    dc_idx = jnp.clip(kc[None, :] - c[:, None] + (NA_WIN_W - 1), 0, 2 * NA_WIN_W - 2)
    delta = jnp.arange(NA_WIN_H)
    j = jnp.arange(NA_WIN_H)
    dr_idx = j[None, :] - delta[:, None] + (NA_WIN_H - 1)
    t = rpb[:, dr_idx[:, :, None, None], dc_idx[None, None, :, :]]
    t = jnp.where(valid[None, None, None], t.astype(jnp.float32), MASK_VALUE)
    t = jnp.transpose(t, (1, 0, 3, 2, 4))
    return t.reshape(NA_WIN_H, N_HEADS, GRID_W, WIN_TOKENS)


def _mixer(x, u, q, k, v, gp, ga, bias, w_pool_bf16, pool_scale, w_out_bf16, g_post):
    batch, seq, _ = x.shape
    n_halo_blocks = seq // HALO_BEFORE
    n_pool_blocks = seq // POOL_HALO
    halo_per_tile = TOKEN_TILE // HALO_BEFORE
    pool_per_tile = TOKEN_TILE // POOL_HALO

    def prev_halo(width):
        return pl.BlockSpec((None, HALO_BEFORE, width),
                            lambda b, i: (b, jnp.maximum(i * halo_per_tile - 1, 0), 0))

    def next_halo(width):
        return pl.BlockSpec((None, HALO_AFTER, width),
                            lambda b, i: (b, jnp.minimum((i + 1) * halo_per_tile, n_halo_blocks - 1), 0))

    in_specs = [
        _tile(D_MODEL),
        _tile(D_POOL),
        pl.BlockSpec((None, POOL_HALO, D_POOL),
                     lambda b, i: (b, jnp.maximum(i * pool_per_tile - 1, 0), 0)),
        pl.BlockSpec((None, POOL_HALO, D_POOL),
                     lambda b, i: (b, jnp.minimum((i + 1) * pool_per_tile, n_pool_blocks - 1), 0)),
        _tile(D_ATT_QK),
        prev_halo(D_ATT_QK), _tile(D_ATT_QK), next_halo(D_ATT_QK),
        prev_halo(D_MODEL), _tile(D_MODEL), next_halo(D_MODEL),
        _tile(D_MODEL), _tile(D_MODEL),
        _resident((NA_WIN_H, N_HEADS, GRID_W, WIN_TOKENS)),
        _resident((POOL_GROUPS, POOL_GROUP_C, POOL_OUT_C)),
        _resident((1, D_MODEL)),
        _resident((D_MODEL, D_MODEL)),
        _resident((1, D_MODEL)),
    ]
    return pl.pallas_call(
        functools.partial(_mixer_kernel, seq=seq),
        grid=(batch, seq // TOKEN_TILE),
        in_specs=in_specs,
        out_specs=_tile(D_MODEL),
        out_shape=jax.ShapeDtypeStruct((batch, seq, D_MODEL), jnp.float32),
        scratch_shapes=[
            pltpu.VMEM((TOKEN_TILE + 2 * POOL_HALO, D_POOL), jnp.float32),
            pltpu.VMEM((KV_WINDOW, D_ATT_QK), jnp.bfloat16),
            pltpu.VMEM((KV_WINDOW, D_MODEL), jnp.bfloat16),
            pltpu.VMEM((TOKEN_TILE, D_MODEL), jnp.float32),
        ],
        compiler_params=pltpu.CompilerParams(
            dimension_semantics=("arbitrary", "arbitrary"),
            vmem_limit_bytes=VMEM_LIMIT_BYTES),
        name="mixer",
    )(x, u, u, u, q, k, k, k, v, v, v, gp, ga, bias, w_pool_bf16, pool_scale, w_out_bf16, g_post)


def _ffn_kernel(x_ref, g_pre_ref, w_gu_ref, w_down_ref, g_post_ref, o_ref, act_ref):
    x = x_ref[...]
    h = _rmsnorm(x, g_pre_ref[...]).astype(jnp.bfloat16)
    for c in range(0, D_FF, FF_CHUNK):
        gate = jnp.dot(h, w_gu_ref[:, c:c + FF_CHUNK], preferred_element_type=jnp.float32)
        up = jnp.dot(h, w_gu_ref[:, D_FF + c:D_FF + c + FF_CHUNK],
                     preferred_element_type=jnp.float32)
        act_ref[:, c:c + FF_CHUNK] = (jax.nn.silu(gate) * up).astype(jnp.bfloat16)
    f = jnp.dot(act_ref[...], w_down_ref[...], preferred_element_type=jnp.float32)
    o_ref[...] = x + _rmsnorm(f, g_post_ref[...])


def _ffn(x, g_pre, w_gu_bf16, w_down_bf16, g_post):
    batch, seq, _ = x.shape
    return pl.pallas_call(
        _ffn_kernel,
        grid=(batch, seq // TOKEN_TILE),
        in_specs=[_tile(D_MODEL), _resident((1, D_MODEL)), _resident((D_MODEL, 2 * D_FF)),
                  _resident((D_FF, D_MODEL)), _resident((1, D_MODEL))],
        out_specs=_tile(D_MODEL),
        out_shape=jax.ShapeDtypeStruct((batch, seq, D_MODEL), jnp.float32),
        scratch_shapes=[pltpu.VMEM((TOKEN_TILE, D_FF), jnp.bfloat16)],
        compiler_params=pltpu.CompilerParams(
            dimension_semantics=("arbitrary", "arbitrary"),
            vmem_limit_bytes=VMEM_LIMIT_BYTES),
        name="ffn",
    )(x, g_pre, w_gu_bf16, w_down_bf16, g_post)


def _layer(x, p):
    assert x.shape[1] % TOKEN_TILE == 0 and x.shape[1] // GRID_W >= NA_WIN_H
    u, q, k, v, gp, ga = _in_proj(x, p["g_mix_pre"], p["w_in"])
    x = _mixer(x, u, q, k, v, gp, ga, p["bias"], p["w_pool"], p["pool_scale"],
               p["w_out"], p["g_mix_post"])
    return _ffn(x, p["g_ffn_pre"], p["w_gu"], p["w_down"], p["g_ffn_post"])


def kernel(x_prompt, x_sample, norm_mix_pre, w_in, w_pool_grp, pool_scale, attn_rpb, w_out,
           norm_mix_post, norm_ffn_pre, w_gate_up, w_down, norm_ffn_post):
    depth = w_in.shape[0]
    bf16 = jnp.bfloat16
    layers = []
    for l in range(depth):
        layers.append(dict(
            g_mix_pre=norm_mix_pre[l][None, :],
            w_in=w_in[l].astype(bf16),
            w_pool=w_pool_grp[l].astype(bf16),
            pool_scale=pool_scale[l][None, :],
            bias=_bias_table(attn_rpb[l]),
            w_out=w_out[l].astype(bf16),
            g_mix_post=norm_mix_post[l][None, :],
            g_ffn_pre=norm_ffn_pre[l][None, :],
            w_gu=w_gate_up[l].astype(bf16),
            w_down=w_down[l].astype(bf16),
            g_ffn_post=norm_ffn_post[l][None, :],
        ))

    def trunk(x):
        for p in layers:
            x = _layer(x, p)
        return x

    return (trunk(x_prompt), trunk(x_sample))
```

```python
import functools

import jax
import jax.numpy as jnp
from jax import lax
from jax.experimental import pallas as pl
from jax.experimental.pallas import tpu as pltpu

D_MODEL = 1024
GRID_W = 64
POOL_WINDOWS = (2, 4, 8, 16)
POOL_GROUPS = 4
D_POOL = D_MODEL // 2
POOL_GROUP_C = D_POOL // POOL_GROUPS
POOL_OUT_C = D_MODEL // POOL_GROUPS
N_HEADS = D_MODEL // 128
D_ATT_QK = D_MODEL // 2
HEAD_DIM_QK = D_ATT_QK // N_HEADS
HEAD_DIM_V = D_MODEL // N_HEADS
NA_WIN_H = 8
NA_WIN_W = 16
D_IN = D_POOL + 2 * D_ATT_QK + D_MODEL + 2 * D_MODEL
D_FF = ((8 * D_MODEL + 3 * 256 - 1) // (3 * 256)) * 256
EPS = 1e-6

O_Q = D_POOL
O_K = O_Q + D_ATT_QK
O_V = O_K + D_ATT_QK
O_GP = O_V + D_MODEL
O_GA = O_GP + D_MODEL

QK_SCALE = HEAD_DIM_QK ** -0.5
MASK_VALUE = -1e30

TOKEN_TILE = 512
ROWS_PER_TILE = TOKEN_TILE // GRID_W
HALO_BEFORE = (NA_WIN_H // 2) * GRID_W
HALO_AFTER = HALO_BEFORE
KV_WINDOW = HALO_BEFORE + TOKEN_TILE + HALO_AFTER
WIN_TOKENS = NA_WIN_H * GRID_W
POOL_HALO = 8
ATTN_ROWS_PER_ITER = 4
FF_CHUNK = 256
VMEM_LIMIT_BYTES = 58 * 1024 * 1024


def _rmsnorm(x, g):
    y = x * lax.rsqrt(jnp.mean(x * x, axis=-1, keepdims=True) + EPS)
    return y * g


def _resident(shape):
    zeros = (0,) * len(shape)
    return pl.BlockSpec(shape, lambda b, i: zeros, pipeline_mode=pl.Buffered(1))


def _tile(width):
    return pl.BlockSpec((None, TOKEN_TILE, width), lambda b, i: (b, i, 0))


def _in_proj_kernel(x_ref, g_ref, w_ref, u_ref, q_ref, k_ref, v_ref, gp_ref, ga_ref):
    h = _rmsnorm(x_ref[...], g_ref[...]).astype(jnp.bfloat16)

    def proj(lo, width):
        return jnp.dot(h, w_ref[:, lo:lo + width], preferred_element_type=jnp.float32)

    u_ref[...] = proj(0, D_POOL)
    q_ref[...] = (proj(O_Q, D_ATT_QK) * QK_SCALE).astype(jnp.bfloat16)
    k_ref[...] = proj(O_K, D_ATT_QK).astype(jnp.bfloat16)
    for c in range(0, D_MODEL, 512):
        v_ref[:, c:c + 512] = proj(O_V + c, 512).astype(jnp.bfloat16)
        gp_ref[:, c:c + 512] = jax.nn.sigmoid(proj(O_GP + c, 512))
        ga_ref[:, c:c + 512] = jax.nn.sigmoid(proj(O_GA + c, 512))


def _in_proj(x, g_pre, w_in_bf16):
    batch, seq, _ = x.shape
    f32, bf16 = jnp.float32, jnp.bfloat16
    out_shape = (
        jax.ShapeDtypeStruct((batch, seq, D_POOL), f32),
        jax.ShapeDtypeStruct((batch, seq, D_ATT_QK), bf16),
        jax.ShapeDtypeStruct((batch, seq, D_ATT_QK), bf16),
        jax.ShapeDtypeStruct((batch, seq, D_MODEL), bf16),
        jax.ShapeDtypeStruct((batch, seq, D_MODEL), f32),
        jax.ShapeDtypeStruct((batch, seq, D_MODEL), f32),
    )
    return pl.pallas_call(
        _in_proj_kernel,
        grid=(batch, seq // TOKEN_TILE),
        in_specs=[_tile(D_MODEL), _resident((1, D_MODEL)), _resident((D_MODEL, D_IN))],
        out_specs=(_tile(D_POOL), _tile(D_ATT_QK), _tile(D_ATT_QK), _tile(D_MODEL),
                   _tile(D_MODEL), _tile(D_MODEL)),
        out_shape=out_shape,
        compiler_params=pltpu.CompilerParams(
            dimension_semantics=("arbitrary", "arbitrary"),
            vmem_limit_bytes=VMEM_LIMIT_BYTES),
        name="in_proj",
    )(x, g_pre, w_in_bf16)


def _mixer_kernel(x_ref, u_ref, u_prev_ref, u_next_ref, q_ref,
                  k_prev_ref, k_ref, k_next_ref, v_prev_ref, v_ref, v_next_ref,
                  gp_ref, ga_ref, bias_ref, w_pool_ref, pool_scale_ref, w_out_ref, g_post_ref,
                  o_ref, upad_ref, kwin_ref, vwin_ref, mix_ref, attn_ref, score_ref, prob_ref, *, seq):
    i = pl.program_id(1)
    n_tiles = pl.num_programs(1)
    rows = seq // GRID_W

    upad_ref[0:POOL_HALO, :] = jnp.where(i > 0, u_prev_ref[...], 0.0)
    upad_ref[POOL_HALO:POOL_HALO + TOKEN_TILE, :] = u_ref[...]
    upad_ref[POOL_HALO + TOKEN_TILE:, :] = jnp.where(i < n_tiles - 1, u_next_ref[...], 0.0)

    t = i * TOKEN_TILE + lax.broadcasted_iota(jnp.int32, (TOKEN_TILE, POOL_GROUP_C), 0)
    for g, w in enumerate(POOL_WINDOWS):
        half = w // 2
        lanes = slice(g * POOL_GROUP_C, (g + 1) * POOL_GROUP_C)
        win_sum = upad_ref[POOL_HALO - half:POOL_HALO - half + TOKEN_TILE, lanes]
        for d in range(-half + 1, half):
            win_sum = win_sum + upad_ref[POOL_HALO + d:POOL_HALO + d + TOKEN_TILE, lanes]
        cnt = jnp.minimum(t + half, seq) - jnp.maximum(t - half, 0)
        pooled = win_sum / cnt.astype(jnp.float32) - u_ref[:, lanes]
        y = jnp.dot(pooled.astype(jnp.bfloat16), w_pool_ref[g],
                    preferred_element_type=jnp.float32)
        cols = slice(g * POOL_OUT_C, (g + 1) * POOL_OUT_C)
        mix_ref[:, cols] = gp_ref[:, cols] * (y * pool_scale_ref[:, cols])

    kwin_ref[0:HALO_BEFORE, :] = k_prev_ref[...]
    kwin_ref[HALO_BEFORE:HALO_BEFORE + TOKEN_TILE, :] = k_ref[...]
    kwin_ref[HALO_BEFORE + TOKEN_TILE:, :] = k_next_ref[...]
    vwin_ref[0:HALO_BEFORE, :] = v_prev_ref[...]
    vwin_ref[HALO_BEFORE:HALO_BEFORE + TOKEN_TILE, :] = v_ref[...]
    vwin_ref[HALO_BEFORE + TOKEN_TILE:, :] = v_next_ref[...]

    lane = lax.broadcasted_iota(jnp.int32, (GRID_W, 128), 1)
    first_row_in_window = i * ROWS_PER_TILE - NA_WIN_H // 2

    def rows_body(it, carry):
        items = []
        for sub in range(ATTN_ROWS_PER_ITER):
            rr = it * ATTN_ROWS_PER_ITER + sub
            r = i * ROWS_PER_TILE + rr
            row_start = jnp.clip(r - NA_WIN_H // 2, 0, rows - NA_WIN_H)
            delta = r - row_start
            start = pl.multiple_of((row_start - first_row_in_window) * GRID_W, GRID_W)
            tok = pl.multiple_of(rr * GRID_W, GRID_W)
            for h in range(N_HEADS):
                items.append((sub * N_HEADS + h, h, delta, start, tok))

        for slot, h, delta, start, tok in items:
            pair = slice((h // 2) * 128, (h // 2 + 1) * 128)
            in_head = (lane >= (h % 2) * HEAD_DIM_QK) & (lane < (h % 2 + 1) * HEAD_DIM_QK)
            q_h = jnp.where(in_head, q_ref[pl.ds(tok, GRID_W), pair], jnp.zeros((), jnp.bfloat16))
            k_h = kwin_ref[pl.ds(start, WIN_TOKENS), pair]
            s = lax.dot_general(q_h, k_h, (((1,), (1,)), ((), ())),
                                preferred_element_type=jnp.float32)
            score_ref[slot] = s + bias_ref[delta, h]
        for slot, h, delta, start, tok in items:
            s = score_ref[slot]
            e = jnp.exp(s - jnp.max(s, axis=-1, keepdims=True))
            p = e / jnp.sum(e, axis=-1, keepdims=True)
            prob_ref[slot] = p.astype(jnp.bfloat16)
        for slot, h, delta, start, tok in items:
            vcols = slice(h * HEAD_DIM_V, (h + 1) * HEAD_DIM_V)
            attn_ref[pl.ds(tok, GRID_W), vcols] = jnp.dot(
                prob_ref[slot], vwin_ref[pl.ds(start, WIN_TOKENS), vcols],
                preferred_element_type=jnp.float32)
        return carry

    lax.fori_loop(0, ROWS_PER_TILE // ATTN_ROWS_PER_ITER, rows_body, 0)

    mixed = mix_ref[...] + ga_ref[...] * attn_ref[...]
    y = jnp.dot(mixed.astype(jnp.bfloat16), w_out_ref[...],
                preferred_element_type=jnp.float32)
    o_ref[...] = x_ref[...] + _rmsnorm(y, g_post_ref[...])


def _bias_table(rpb):
    n_dr = 2 * NA_WIN_H - 1
    period = 2 * GRID_W - 1
    rpb = rpb.astype(jnp.float32)
    ring = jnp.concatenate(
        [rpb[..., NA_WIN_W - 1:],
         jnp.full((N_HEADS, n_dr, period - (2 * NA_WIN_W - 1)), MASK_VALUE, jnp.float32),
         rpb[..., :NA_WIN_W - 1]], axis=-1)
    tiled = jnp.tile(ring, (1, 1, GRID_W))[..., :GRID_W * (period - 1)]
    toeplitz = tiled.reshape(N_HEADS, n_dr, GRID_W, period - 1)[..., :GRID_W]
    c = jnp.arange(GRID_W)
    col_start = jnp.clip(c - NA_WIN_W // 2, 0, GRID_W - NA_WIN_W)
    kc = jnp.arange(GRID_W)
    valid = (kc[None, :] >= col_start[:, None]) & (kc[None, :] < col_start[:, None] + NA_WIN_W)
    toeplitz = jnp.where(valid[None, None], toeplitz, MASK_VALUE)
    t = jnp.stack([toeplitz[:, NA_WIN_H - 1 - d:2 * NA_WIN_H - 1 - d] for d in range(NA_WIN_H)])
    t = jnp.transpose(t, (0, 1, 3, 2, 4))
    return t.reshape(NA_WIN_H, N_HEADS, GRID_W, WIN_TOKENS)


def _mixer(x, u, q, k, v, gp, ga, bias, w_pool_bf16, pool_scale, w_out_bf16, g_post):
    batch, seq, _ = x.shape
    n_halo_blocks = seq // HALO_BEFORE
    n_pool_blocks = seq // POOL_HALO
    halo_per_tile = TOKEN_TILE // HALO_BEFORE
    pool_per_tile = TOKEN_TILE // POOL_HALO

    def prev_halo(width):
        return pl.BlockSpec((None, HALO_BEFORE, width),
                            lambda b, i: (b, jnp.maximum(i * halo_per_tile - 1, 0), 0))

    def next_halo(width):
        return pl.BlockSpec((None, HALO_AFTER, width),
                            lambda b, i: (b, jnp.minimum((i + 1) * halo_per_tile, n_halo_blocks - 1), 0))

    in_specs = [
        _tile(D_MODEL),
        _tile(D_POOL),
        pl.BlockSpec((None, POOL_HALO, D_POOL),
                     lambda b, i: (b, jnp.maximum(i * pool_per_tile - 1, 0), 0)),
        pl.BlockSpec((None, POOL_HALO, D_POOL),
                     lambda b, i: (b, jnp.minimum((i + 1) * pool_per_tile, n_pool_blocks - 1), 0)),
        _tile(D_ATT_QK),
        prev_halo(D_ATT_QK), _tile(D_ATT_QK), next_halo(D_ATT_QK),
        prev_halo(D_MODEL), _tile(D_MODEL), next_halo(D_MODEL),
        _tile(D_MODEL), _tile(D_MODEL),
        _resident((NA_WIN_H, N_HEADS, GRID_W, WIN_TOKENS)),
        _resident((POOL_GROUPS, POOL_GROUP_C, POOL_OUT_C)),
        _resident((1, D_MODEL)),
        _resident((D_MODEL, D_MODEL)),
        _resident((1, D_MODEL)),
    ]
    return pl.pallas_call(
        functools.partial(_mixer_kernel, seq=seq),
        grid=(batch, seq // TOKEN_TILE),
        in_specs=in_specs,
        out_specs=_tile(D_MODEL),
        out_shape=jax.ShapeDtypeStruct((batch, seq, D_MODEL), jnp.float32),
        scratch_shapes=[
            pltpu.VMEM((TOKEN_TILE + 2 * POOL_HALO, D_POOL), jnp.float32),
            pltpu.VMEM((KV_WINDOW, D_ATT_QK), jnp.bfloat16),
            pltpu.VMEM((KV_WINDOW, D_MODEL), jnp.bfloat16),
            pltpu.VMEM((TOKEN_TILE, D_MODEL), jnp.float32),
            pltpu.VMEM((TOKEN_TILE, D_MODEL), jnp.float32),
            pltpu.VMEM((ATTN_ROWS_PER_ITER * N_HEADS, GRID_W, WIN_TOKENS), jnp.float32),
            pltpu.VMEM((ATTN_ROWS_PER_ITER * N_HEADS, GRID_W, WIN_TOKENS), jnp.bfloat16),
        ],
        compiler_params=pltpu.CompilerParams(
            dimension_semantics=("arbitrary", "arbitrary"),
            vmem_limit_bytes=VMEM_LIMIT_BYTES),
        name="mixer",
    )(x, u, u, u, q, k, k, k, v, v, v, gp, ga, bias, w_pool_bf16, pool_scale, w_out_bf16, g_post)


def _ffn_kernel(x_ref, g_pre_ref, w_gu_ref, w_down_ref, g_post_ref, o_ref, act_ref):
    x = x_ref[...]
    h = _rmsnorm(x, g_pre_ref[...]).astype(jnp.bfloat16)
    for c in range(0, D_FF, FF_CHUNK):
        gate = jnp.dot(h, w_gu_ref[:, c:c + FF_CHUNK], preferred_element_type=jnp.float32)
        up = jnp.dot(h, w_gu_ref[:, D_FF + c:D_FF + c + FF_CHUNK],
                     preferred_element_type=jnp.float32)
        act_ref[:, c:c + FF_CHUNK] = (jax.nn.silu(gate) * up).astype(jnp.bfloat16)
    f = jnp.dot(act_ref[...], w_down_ref[...], preferred_element_type=jnp.float32)
    o_ref[...] = x + _rmsnorm(f, g_post_ref[...])


def _ffn(x, g_pre, w_gu_bf16, w_down_bf16, g_post):
    batch, seq, _ = x.shape
    return pl.pallas_call(
        _ffn_kernel,
        grid=(batch, seq // TOKEN_TILE),
        in_specs=[_tile(D_MODEL), _resident((1, D_MODEL)), _resident((D_MODEL, 2 * D_FF)),
                  _resident((D_FF, D_MODEL)), _resident((1, D_MODEL))],
        out_specs=_tile(D_MODEL),
        out_shape=jax.ShapeDtypeStruct((batch, seq, D_MODEL), jnp.float32),
        scratch_shapes=[pltpu.VMEM((TOKEN_TILE, D_FF), jnp.bfloat16)],
        compiler_params=pltpu.CompilerParams(
            dimension_semantics=("arbitrary", "arbitrary"),
            vmem_limit_bytes=VMEM_LIMIT_BYTES),
        name="ffn",
    )(x, g_pre, w_gu_bf16, w_down_bf16, g_post)


def _layer(x, p):
    assert x.shape[1] % TOKEN_TILE == 0 and x.shape[1] // GRID_W >= NA_WIN_H
    u, q, k, v, gp, ga = _in_proj(x, p["g_mix_pre"], p["w_in"])
    x = _mixer(x, u, q, k, v, gp, ga, p["bias"], p["w_pool"], p["pool_scale"],
               p["w_out"], p["g_mix_post"])
    return _ffn(x, p["g_ffn_pre"], p["w_gu"], p["w_down"], p["g_ffn_post"])


def kernel(x_prompt, x_sample, norm_mix_pre, w_in, w_pool_grp, pool_scale, attn_rpb, w_out,
           norm_mix_post, norm_ffn_pre, w_gate_up, w_down, norm_ffn_post):
    depth = w_in.shape[0]
    bf16 = jnp.bfloat16
    layers = []
    for l in range(depth):
        layers.append(dict(
            g_mix_pre=norm_mix_pre[l][None, :],
            w_in=w_in[l].astype(bf16),
            w_pool=w_pool_grp[l].astype(bf16),
            pool_scale=pool_scale[l][None, :],
            bias=_bias_table(attn_rpb[l]),
            w_out=w_out[l].astype(bf16),
            g_mix_post=norm_mix_post[l][None, :],
            g_ffn_pre=norm_ffn_pre[l][None, :],
            w_gu=w_gate_up[l].astype(bf16),
            w_down=w_down[l].astype(bf16),
            g_ffn_post=norm_ffn_post[l][None, :],
        ))

    def trunk(x):
        for p in layers:
            x = _layer(x, p)
        return x

    return (trunk(x_prompt), trunk(x_sample))
```

```python
import functools

import jax
import jax.numpy as jnp
from jax import lax
from jax.experimental import pallas as pl
from jax.experimental.pallas import tpu as pltpu

D_MODEL = 1024
GRID_W = 64
POOL_WINDOWS = (2, 4, 8, 16)
POOL_GROUPS = 4
D_POOL = D_MODEL // 2
POOL_GROUP_C = D_POOL // POOL_GROUPS
POOL_OUT_C = D_MODEL // POOL_GROUPS
N_HEADS = D_MODEL // 128
D_ATT_QK = D_MODEL // 2
HEAD_DIM_QK = D_ATT_QK // N_HEADS
HEAD_DIM_V = D_MODEL // N_HEADS
NA_WIN_H = 8
NA_WIN_W = 16
D_IN = D_POOL + 2 * D_ATT_QK + D_MODEL + 2 * D_MODEL
D_FF = ((8 * D_MODEL + 3 * 256 - 1) // (3 * 256)) * 256
EPS = 1e-6

O_Q = D_POOL
O_K = O_Q + D_ATT_QK
O_V = O_K + D_ATT_QK
O_GP = O_V + D_MODEL
O_GA = O_GP + D_MODEL

QK_SCALE = HEAD_DIM_QK ** -0.5
MASK_VALUE = -1e30

TOKEN_TILE = 512
ROWS_PER_TILE = TOKEN_TILE // GRID_W
KV_HALO = (NA_WIN_H // 2) * GRID_W
KV_WINDOW = KV_HALO + TOKEN_TILE + KV_HALO
WIN_TOKENS = NA_WIN_H * GRID_W
POOL_HALO = 16
ATTN_ROWS_PER_ITER = 4
FF_CHUNK = 256
VMEM_LIMIT_BYTES = 58 * 1024 * 1024


def _rmsnorm(x, g):
    y = x * lax.rsqrt(jnp.mean(x * x, axis=-1, keepdims=True) + EPS)
    return y * g


def _resident(shape):
    zeros = (0,) * len(shape)
    return pl.BlockSpec(shape, lambda b, i: zeros, pipeline_mode=pl.Buffered(1))


def _tile(width):
    return pl.BlockSpec((None, TOKEN_TILE, width), lambda b, i: (b, i, 0))


def _params():
    return pltpu.CompilerParams(dimension_semantics=("arbitrary", "arbitrary"),
                                vmem_limit_bytes=VMEM_LIMIT_BYTES)


def _kv_proj_kernel(x_ref, g_ref, w_ref, k_ref, v_ref):
    h = _rmsnorm(x_ref[...], g_ref[...]).astype(jnp.bfloat16)
    k_ref[...] = jnp.dot(h, w_ref[:, :D_ATT_QK],
                         preferred_element_type=jnp.float32).astype(jnp.bfloat16)
    v_ref[...] = jnp.dot(h, w_ref[:, D_ATT_QK:],
                         preferred_element_type=jnp.float32).astype(jnp.bfloat16)


def _kv_proj(x, g_pre, w_kv):
    batch, seq, _ = x.shape
    return pl.pallas_call(
        _kv_proj_kernel,
        grid=(batch, seq // TOKEN_TILE),
        in_specs=[_tile(D_MODEL), _resident((1, D_MODEL)),
                  _resident((D_MODEL, D_ATT_QK + D_MODEL))],
        out_specs=(_tile(D_ATT_QK), _tile(D_MODEL)),
        out_shape=(jax.ShapeDtypeStruct((batch, seq, D_ATT_QK), jnp.bfloat16),
                   jax.ShapeDtypeStruct((batch, seq, D_MODEL), jnp.bfloat16)),
        compiler_params=_params(),
        name="kv_proj",
    )(x, g_pre, w_kv)


def _mixer_kernel(x_ref, x_prev_ref, x_next_ref, k_prev_ref, k_cur_ref, k_next_ref,
                  v_prev_ref, v_cur_ref, v_next_ref, g_pre_ref, w_uq_ref, w_gate_ref,
                  bias_ref, w_pool_ref, pool_scale_ref, w_out_ref, g_post_ref,
                  o_ref, h_ref, upad_ref, q_ref, ga_ref, mix_ref, attn_ref, score_ref, prob_ref,
                  k_ref, v_ref, *, seq):
    i = pl.program_id(1)
    rows = seq // GRID_W
    tile_rows = slice(POOL_HALO, POOL_HALO + TOKEN_TILE)

    g_pre = g_pre_ref[...]
    h_ref[0:POOL_HALO, :] = _rmsnorm(x_prev_ref[...], g_pre).astype(jnp.bfloat16)
    h_ref[tile_rows, :] = _rmsnorm(x_ref[...], g_pre).astype(jnp.bfloat16)
    h_ref[POOL_HALO + TOKEN_TILE:, :] = _rmsnorm(x_next_ref[...], g_pre).astype(jnp.bfloat16)

    u_ext = jnp.dot(h_ref[...], w_uq_ref[:, :D_POOL], preferred_element_type=jnp.float32)
    tok = (i * TOKEN_TILE - POOL_HALO
           + lax.broadcasted_iota(jnp.int32, (TOKEN_TILE + 2 * POOL_HALO, D_POOL), 0))
    upad_ref[...] = jnp.where((tok >= 0) & (tok < seq), u_ext, 0.0)
    h = h_ref[tile_rows, :]
    q_ref[...] = (jnp.dot(h, w_uq_ref[:, D_POOL:], preferred_element_type=jnp.float32)
                  * QK_SCALE).astype(jnp.bfloat16)
    for c in range(0, D_MODEL, 512):
        ga_ref[:, c:c + 512] = jax.nn.sigmoid(
            jnp.dot(h, w_gate_ref[:, D_MODEL + c:D_MODEL + c + 512],
                    preferred_element_type=jnp.float32))

    t = i * TOKEN_TILE + lax.broadcasted_iota(jnp.int32, (TOKEN_TILE, POOL_GROUP_C), 0)
    for g, w in enumerate(POOL_WINDOWS):
        half = w // 2
        lanes = slice(g * POOL_GROUP_C, (g + 1) * POOL_GROUP_C)
        win_sum = upad_ref[POOL_HALO - half:POOL_HALO - half + TOKEN_TILE, lanes]
        for d in range(-half + 1, half):
            win_sum = win_sum + upad_ref[POOL_HALO + d:POOL_HALO + d + TOKEN_TILE, lanes]
        cnt = jnp.minimum(t + half, seq) - jnp.maximum(t - half, 0)
        pooled = win_sum / cnt.astype(jnp.float32) - upad_ref[tile_rows, lanes]
        y = jnp.dot(pooled.astype(jnp.bfloat16), w_pool_ref[g],
                    preferred_element_type=jnp.float32)
        cols = slice(g * POOL_OUT_C, (g + 1) * POOL_OUT_C)
        gate = jax.nn.sigmoid(jnp.dot(h, w_gate_ref[:, cols], preferred_element_type=jnp.float32))
        mix_ref[:, cols] = gate * (y * pool_scale_ref[:, cols])

    lane = lax.broadcasted_iota(jnp.int32, (GRID_W, 128), 1)
    for src, dst in ((k_prev_ref, k_ref), (v_prev_ref, v_ref)):
        dst[0:KV_HALO, :] = src[...]
    for src, dst in ((k_cur_ref, k_ref), (v_cur_ref, v_ref)):
        dst[KV_HALO:KV_HALO + TOKEN_TILE, :] = src[...]
    for src, dst in ((k_next_ref, k_ref), (v_next_ref, v_ref)):
        dst[KV_HALO + TOKEN_TILE:, :] = src[...]
    first_row_in_window = i * ROWS_PER_TILE - KV_HALO // GRID_W

    def rows_body(it, carry):
        items = []
        for sub in range(ATTN_ROWS_PER_ITER):
            rr = it * ATTN_ROWS_PER_ITER + sub
            r = i * ROWS_PER_TILE + rr
            row_start = jnp.clip(r - NA_WIN_H // 2, 0, rows - NA_WIN_H)
            delta = r - row_start
            start = pl.multiple_of((row_start - first_row_in_window) * GRID_W, GRID_W)
            tok_q = pl.multiple_of(rr * GRID_W, GRID_W)
            for hd in range(N_HEADS):
                items.append((sub * N_HEADS + hd, hd, delta, start, tok_q))

        for slot, hd, delta, start, tok_q in items:
            pair = slice((hd // 2) * 128, (hd // 2 + 1) * 128)
            in_head = (lane >= (hd % 2) * HEAD_DIM_QK) & (lane < (hd % 2 + 1) * HEAD_DIM_QK)
            q_h = jnp.where(in_head, q_ref[pl.ds(tok_q, GRID_W), pair],
                            jnp.zeros((), jnp.bfloat16))
            k_h = k_ref[pl.ds(start, WIN_TOKENS), pair]
            s = lax.dot_general(q_h, k_h, (((1,), (1,)), ((), ())),
                                preferred_element_type=jnp.float32)
            score_ref[slot] = s + bias_ref[delta, hd]
        for slot, hd, delta, start, tok_q in items:
            s = score_ref[slot]
            e = jnp.exp(s - jnp.max(s, axis=-1, keepdims=True))
            p = e / jnp.sum(e, axis=-1, keepdims=True)
            prob_ref[slot] = p.astype(jnp.bfloat16)
        for slot, hd, delta, start, tok_q in items:
            vcols = slice(hd * HEAD_DIM_V, (hd + 1) * HEAD_DIM_V)
            attn_ref[pl.ds(tok_q, GRID_W), vcols] = jnp.dot(
                prob_ref[slot], v_ref[pl.ds(start, WIN_TOKENS), vcols],
                preferred_element_type=jnp.float32)
        return carry

    lax.fori_loop(0, ROWS_PER_TILE // ATTN_ROWS_PER_ITER, rows_body, 0)

    mixed = mix_ref[...] + ga_ref[...] * attn_ref[...]
    y = jnp.dot(mixed.astype(jnp.bfloat16), w_out_ref[...],
                preferred_element_type=jnp.float32)
    o_ref[...] = x_ref[...] + _rmsnorm(y, g_post_ref[...])


def _bias_table(rpb):
    n_dr = 2 * NA_WIN_H - 1
    period = 2 * GRID_W - 1
    rpb = rpb.astype(jnp.float32)
    ring = jnp.concatenate(
        [rpb[..., NA_WIN_W - 1:],
         jnp.full((N_HEADS, n_dr, period - (2 * NA_WIN_W - 1)), MASK_VALUE, jnp.float32),
         rpb[..., :NA_WIN_W - 1]], axis=-1)
    tiled = jnp.tile(ring, (1, 1, GRID_W))[..., :GRID_W * (period - 1)]
    toeplitz = tiled.reshape(N_HEADS, n_dr, GRID_W, period - 1)[..., :GRID_W]
    c = jnp.arange(GRID_W)
    col_start = jnp.clip(c - NA_WIN_W // 2, 0, GRID_W - NA_WIN_W)
    kc = jnp.arange(GRID_W)
    valid = (kc[None, :] >= col_start[:, None]) & (kc[None, :] < col_start[:, None] + NA_WIN_W)
    toeplitz = jnp.where(valid[None, None], toeplitz, MASK_VALUE)
    t = jnp.stack([toeplitz[:, NA_WIN_H - 1 - d:2 * NA_WIN_H - 1 - d] for d in range(NA_WIN_H)])
    t = jnp.transpose(t, (0, 1, 3, 2, 4))
    return t.reshape(NA_WIN_H, N_HEADS, GRID_W, WIN_TOKENS)


def _mixer(x, k, v, g_pre, w_uq, w_gate, bias, w_pool, pool_scale, w_out, g_post):
    batch, seq, _ = x.shape

    def prev_halo(halo, width):
        per_tile = TOKEN_TILE // halo
        return pl.BlockSpec((None, halo, width),
                            lambda b, i: (b, jnp.maximum(i * per_tile - 1, 0), 0))

    def next_halo(halo, width):
        per_tile = TOKEN_TILE // halo
        last = seq // halo - 1
        return pl.BlockSpec((None, halo, width),
                            lambda b, i: (b, jnp.minimum((i + 1) * per_tile, last), 0))

    in_specs = [
        _tile(D_MODEL), prev_halo(POOL_HALO, D_MODEL), next_halo(POOL_HALO, D_MODEL),
        prev_halo(KV_HALO, D_ATT_QK), _tile(D_ATT_QK), next_halo(KV_HALO, D_ATT_QK),
        prev_halo(KV_HALO, D_MODEL), _tile(D_MODEL), next_halo(KV_HALO, D_MODEL),
        _resident((1, D_MODEL)),
        _resident((D_MODEL, D_POOL + D_ATT_QK)),
        _resident((D_MODEL, 2 * D_MODEL)),
        _resident((NA_WIN_H, N_HEADS, GRID_W, WIN_TOKENS)),
        _resident((POOL_GROUPS, POOL_GROUP_C, POOL_OUT_C)),
        _resident((1, D_MODEL)),
        _resident((D_MODEL, D_MODEL)),
        _resident((1, D_MODEL)),
    ]
    n_slots = ATTN_ROWS_PER_ITER * N_HEADS
    return pl.pallas_call(
        functools.partial(_mixer_kernel, seq=seq),
        grid=(batch, seq // TOKEN_TILE),
        in_specs=in_specs,
        out_specs=_tile(D_MODEL),
        out_shape=jax.ShapeDtypeStruct((batch, seq, D_MODEL), jnp.float32),
        scratch_shapes=[
            pltpu.VMEM((TOKEN_TILE + 2 * POOL_HALO, D_MODEL), jnp.bfloat16),
            pltpu.VMEM((TOKEN_TILE + 2 * POOL_HALO, D_POOL), jnp.float32),
            pltpu.VMEM((TOKEN_TILE, D_ATT_QK), jnp.bfloat16),
            pltpu.VMEM((TOKEN_TILE, D_MODEL), jnp.float32),
            pltpu.VMEM((TOKEN_TILE, D_MODEL), jnp.float32),
            pltpu.VMEM((TOKEN_TILE, D_MODEL), jnp.float32),
            pltpu.VMEM((n_slots, GRID_W, WIN_TOKENS), jnp.float32),
            pltpu.VMEM((n_slots, GRID_W, WIN_TOKENS), jnp.bfloat16),
            pltpu.VMEM((KV_WINDOW, D_ATT_QK), jnp.bfloat16),
            pltpu.VMEM((KV_WINDOW, D_MODEL), jnp.bfloat16),
        ],
        compiler_params=_params(),
        name="mixer",
    )(x, x, x, k, k, k, v, v, v, g_pre, w_uq, w_gate, bias, w_pool, pool_scale, w_out, g_post)


def _ffn_kernel(x_ref, g_pre_ref, w_gu_ref, w_down_ref, g_post_ref, o_ref, act_ref):
    x = x_ref[...]
    h = _rmsnorm(x, g_pre_ref[...]).astype(jnp.bfloat16)
    for c in range(0, D_FF, FF_CHUNK):
        gate = jnp.dot(h, w_gu_ref[:, c:c + FF_CHUNK], preferred_element_type=jnp.float32)
        up = jnp.dot(h, w_gu_ref[:, D_FF + c:D_FF + c + FF_CHUNK],
                     preferred_element_type=jnp.float32)
        act_ref[:, c:c + FF_CHUNK] = (jax.nn.silu(gate) * up).astype(jnp.bfloat16)
    f = jnp.dot(act_ref[...], w_down_ref[...], preferred_element_type=jnp.float32)
    o_ref[...] = x + _rmsnorm(f, g_post_ref[...])


def _ffn(x, g_pre, w_gu, w_down, g_post):
    batch, seq, _ = x.shape
    return pl.pallas_call(
        _ffn_kernel,
        grid=(batch, seq // TOKEN_TILE),
        in_specs=[_tile(D_MODEL), _resident((1, D_MODEL)), _resident((D_MODEL, 2 * D_FF)),
                  _resident((D_FF, D_MODEL)), _resident((1, D_MODEL))],
        out_specs=_tile(D_MODEL),
        out_shape=jax.ShapeDtypeStruct((batch, seq, D_MODEL), jnp.float32),
        scratch_shapes=[pltpu.VMEM((TOKEN_TILE, D_FF), jnp.bfloat16)],
        compiler_params=_params(),
        name="ffn",
    )(x, g_pre, w_gu, w_down, g_post)


def _layer(x, p):
    seq = x.shape[1]
    assert seq % TOKEN_TILE == 0 and seq >= KV_WINDOW and seq // GRID_W >= NA_WIN_H
    k, v = _kv_proj(x, p["g_mix_pre"], p["w_kv"])
    x = _mixer(x, k, v, p["g_mix_pre"], p["w_uq"], p["w_gate"], p["bias"], p["w_pool"],
               p["pool_scale"], p["w_out"], p["g_mix_post"])
    return _ffn(x, p["g_ffn_pre"], p["w_gu"], p["w_down"], p["g_ffn_post"])


def kernel(x_prompt, x_sample, norm_mix_pre, w_in, w_pool_grp, pool_scale, attn_rpb, w_out,
           norm_mix_post, norm_ffn_pre, w_gate_up, w_down, norm_ffn_post):
    depth = w_in.shape[0]
    bf16 = jnp.bfloat16
    layers = []
    for l in range(depth):
        w_in_l = w_in[l].astype(bf16)
        layers.append(dict(
            g_mix_pre=norm_mix_pre[l][None, :],
            w_uq=w_in_l[:, :O_K],
            w_kv=w_in_l[:, O_K:O_GP],
            w_gate=w_in_l[:, O_GP:],
            w_pool=w_pool_grp[l].astype(bf16),
            pool_scale=pool_scale[l][None, :],
            bias=_bias_table(attn_rpb[l]),
            w_out=w_out[l].astype(bf16),
            g_mix_post=norm_mix_post[l][None, :],
            g_ffn_pre=norm_ffn_pre[l][None, :],
            w_gu=w_gate_up[l].astype(bf16),
            w_down=w_down[l].astype(bf16),
            g_ffn_post=norm_ffn_post[l][None, :],
        ))

    def trunk(x):
        for p in layers:
            x = _layer(x, p)
        return x

    return (trunk(x_prompt), trunk(x_sample))
```

```python
import functools

import jax
import jax.numpy as jnp
from jax import lax
from jax.experimental import pallas as pl
from jax.experimental.pallas import tpu as pltpu

D_MODEL = 1024
GRID_W = 64
POOL_WINDOWS = (2, 4, 8, 16)
POOL_GROUPS = 4
D_POOL = D_MODEL // 2
POOL_GROUP_C = D_POOL // POOL_GROUPS
POOL_OUT_C = D_MODEL // POOL_GROUPS
N_HEADS = D_MODEL // 128
D_ATT_QK = D_MODEL // 2
HEAD_DIM_QK = D_ATT_QK // N_HEADS
HEAD_DIM_V = D_MODEL // N_HEADS
NA_WIN_H = 8
NA_WIN_W = 16
D_IN = D_POOL + 2 * D_ATT_QK + D_MODEL + 2 * D_MODEL
D_FF = ((8 * D_MODEL + 3 * 256 - 1) // (3 * 256)) * 256
EPS = 1e-6

O_Q = D_POOL
O_K = O_Q + D_ATT_QK
O_V = O_K + D_ATT_QK
O_GP = O_V + D_MODEL
O_GA = O_GP + D_MODEL

QK_SCALE = HEAD_DIM_QK ** -0.5
MASK_VALUE = -1e30

TOKEN_TILE = 512
KV_PROJ_TILE = 1024
FFN_TILE = 1024
FFN_SUBTILE = 512
ROWS_PER_TILE = TOKEN_TILE // GRID_W
KV_HALO = (NA_WIN_H // 2) * GRID_W
KV_WINDOW = KV_HALO + TOKEN_TILE + KV_HALO
WIN_TOKENS = NA_WIN_H * GRID_W
POOL_HALO = 16
ATTN_ROWS_PER_ITER = 4
FF_CHUNK = 256
VMEM_LIMIT_BYTES = 58 * 1024 * 1024


def _rmsnorm(x, g):
    y = x * lax.rsqrt(jnp.mean(x * x, axis=-1, keepdims=True) + EPS)
    return y * g


def _resident(shape):
    zeros = (0,) * len(shape)
    return pl.BlockSpec(shape, lambda b, i: zeros, pipeline_mode=pl.Buffered(1))


def _tile(width, tokens=TOKEN_TILE):
    return pl.BlockSpec((None, tokens, width), lambda b, i: (b, i, 0))


def _params():
    return pltpu.CompilerParams(dimension_semantics=("arbitrary", "arbitrary"),
                                vmem_limit_bytes=VMEM_LIMIT_BYTES)


def _kv_proj_kernel(x_ref, g_ref, w_ref, k_ref, v_ref):
    h = _rmsnorm(x_ref[...], g_ref[...]).astype(jnp.bfloat16)
    k_ref[...] = jnp.dot(h, w_ref[:, :D_ATT_QK],
                         preferred_element_type=jnp.float32).astype(jnp.bfloat16)
    v_ref[...] = jnp.dot(h, w_ref[:, D_ATT_QK:],
                         preferred_element_type=jnp.float32).astype(jnp.bfloat16)


def _kv_proj(x, g_pre, w_kv):
    batch, seq, _ = x.shape
    return pl.pallas_call(
        _kv_proj_kernel,
        grid=(batch, seq // KV_PROJ_TILE),
        in_specs=[_tile(D_MODEL, KV_PROJ_TILE), _resident((1, D_MODEL)),
                  _resident((D_MODEL, D_ATT_QK + D_MODEL))],
        out_specs=(_tile(D_ATT_QK, KV_PROJ_TILE), _tile(D_MODEL, KV_PROJ_TILE)),
        out_shape=(jax.ShapeDtypeStruct((batch, seq, D_ATT_QK), jnp.bfloat16),
                   jax.ShapeDtypeStruct((batch, seq, D_MODEL), jnp.bfloat16)),
        compiler_params=_params(),
        name="kv_proj",
    )(x, g_pre, w_kv)


def _mixer_kernel(x_ref, x_prev_ref, x_next_ref, k_prev_ref, k_cur_ref, k_next_ref,
                  v_prev_ref, v_cur_ref, v_next_ref, g_pre_ref, w_uq_ref, w_gate_ref,
                  bias_ref, w_pool_ref, pool_scale_ref, w_out_ref, g_post_ref,
                  o_ref, h_ref, upad_ref, q_ref, ga_ref, mix_ref, attn_ref, score_ref, prob_ref,
                  k_ref, v_ref, *, seq):
    i = pl.program_id(1)
    rows = seq // GRID_W
    tile_rows = slice(POOL_HALO, POOL_HALO + TOKEN_TILE)

    g_pre = g_pre_ref[...]
    h_ref[0:POOL_HALO, :] = _rmsnorm(x_prev_ref[...], g_pre).astype(jnp.bfloat16)
    h_ref[tile_rows, :] = _rmsnorm(x_ref[...], g_pre).astype(jnp.bfloat16)
    h_ref[POOL_HALO + TOKEN_TILE:, :] = _rmsnorm(x_next_ref[...], g_pre).astype(jnp.bfloat16)

    u_ext = jnp.dot(h_ref[...], w_uq_ref[:, :D_POOL], preferred_element_type=jnp.float32)
    tok = (i * TOKEN_TILE - POOL_HALO
           + lax.broadcasted_iota(jnp.int32, (TOKEN_TILE + 2 * POOL_HALO, D_POOL), 0))
    upad_ref[...] = jnp.where((tok >= 0) & (tok < seq), u_ext, 0.0)
    h = h_ref[tile_rows, :]
    q_ref[...] = (jnp.dot(h, w_uq_ref[:, D_POOL:], preferred_element_type=jnp.float32)
                  * QK_SCALE).astype(jnp.bfloat16)
    for c in range(0, D_MODEL, 512):
        ga_ref[:, c:c + 512] = jax.nn.sigmoid(
            jnp.dot(h, w_gate_ref[:, D_MODEL + c:D_MODEL + c + 512],
                    preferred_element_type=jnp.float32))

    t = i * TOKEN_TILE + lax.broadcasted_iota(jnp.int32, (TOKEN_TILE, POOL_GROUP_C), 0)
    for g, w in enumerate(POOL_WINDOWS):
        half = w // 2
        lanes = slice(g * POOL_GROUP_C, (g + 1) * POOL_GROUP_C)
        win_sum = upad_ref[POOL_HALO - half:POOL_HALO - half + TOKEN_TILE, lanes]
        for d in range(-half + 1, half):
            win_sum = win_sum + upad_ref[POOL_HALO + d:POOL_HALO + d + TOKEN_TILE, lanes]
        cnt = jnp.minimum(t + half, seq) - jnp.maximum(t - half, 0)
        pooled = win_sum / cnt.astype(jnp.float32) - upad_ref[tile_rows, lanes]
        y = jnp.dot(pooled.astype(jnp.bfloat16), w_pool_ref[g],
                    preferred_element_type=jnp.float32)
        cols = slice(g * POOL_OUT_C, (g + 1) * POOL_OUT_C)
        gate = jax.nn.sigmoid(jnp.dot(h, w_gate_ref[:, cols], preferred_element_type=jnp.float32))
        mix_ref[:, cols] = gate * (y * pool_scale_ref[:, cols])

    lane = lax.broadcasted_iota(jnp.int32, (GRID_W, 128), 1)
    for src, dst in ((k_prev_ref, k_ref), (v_prev_ref, v_ref)):
        dst[0:KV_HALO, :] = src[...]
    for src, dst in ((k_cur_ref, k_ref), (v_cur_ref, v_ref)):
        dst[KV_HALO:KV_HALO + TOKEN_TILE, :] = src[...]
    for src, dst in ((k_next_ref, k_ref), (v_next_ref, v_ref)):
        dst[KV_HALO + TOKEN_TILE:, :] = src[...]
    first_row_in_window = i * ROWS_PER_TILE - KV_HALO // GRID_W

    def rows_body(it, carry):
        items = []
        for sub in range(ATTN_ROWS_PER_ITER):
            rr = it * ATTN_ROWS_PER_ITER + sub
            r = i * ROWS_PER_TILE + rr
            row_start = jnp.clip(r - NA_WIN_H // 2, 0, rows - NA_WIN_H)
            delta = r - row_start
            start = pl.multiple_of((row_start - first_row_in_window) * GRID_W, GRID_W)
            tok_q = pl.multiple_of(rr * GRID_W, GRID_W)
            for hd in range(N_HEADS):
                items.append((sub * N_HEADS + hd, hd, delta, start, tok_q))

        for slot, hd, delta, start, tok_q in items:
            pair = slice((hd // 2) * 128, (hd // 2 + 1) * 128)
            in_head = (lane >= (hd % 2) * HEAD_DIM_QK) & (lane < (hd % 2 + 1) * HEAD_DIM_QK)
            q_h = jnp.where(in_head, q_ref[pl.ds(tok_q, GRID_W), pair],
                            jnp.zeros((), jnp.bfloat16))
            k_h = k_ref[pl.ds(start, WIN_TOKENS), pair]
            s = lax.dot_general(q_h, k_h, (((1,), (1,)), ((), ())),
                                preferred_element_type=jnp.float32)
            score_ref[slot] = s + bias_ref[delta, hd]
        for slot, hd, delta, start, tok_q in items:
            s = score_ref[slot]
            e = jnp.exp(s - jnp.max(s, axis=-1, keepdims=True))
            p = e / jnp.sum(e, axis=-1, keepdims=True)
            prob_ref[slot] = p.astype(jnp.bfloat16)
        for slot, hd, delta, start, tok_q in items:
            vcols = slice(hd * HEAD_DIM_V, (hd + 1) * HEAD_DIM_V)
            attn_ref[pl.ds(tok_q, GRID_W), vcols] = jnp.dot(
                prob_ref[slot], v_ref[pl.ds(start, WIN_TOKENS), vcols],
                preferred_element_type=jnp.float32)
        return carry

    lax.fori_loop(0, ROWS_PER_TILE // ATTN_ROWS_PER_ITER, rows_body, 0)

    mixed = mix_ref[...] + ga_ref[...] * attn_ref[...]
    y = jnp.dot(mixed.astype(jnp.bfloat16), w_out_ref[...],
                preferred_element_type=jnp.float32)
    o_ref[...] = x_ref[...] + _rmsnorm(y, g_post_ref[...])


def _bias_table(rpb):
    n_dr = 2 * NA_WIN_H - 1
    period = 2 * GRID_W - 1
    rpb = rpb.astype(jnp.float32)
    ring = jnp.concatenate(
        [rpb[..., NA_WIN_W - 1:],
         jnp.full((N_HEADS, n_dr, period - (2 * NA_WIN_W - 1)), MASK_VALUE, jnp.float32),
         rpb[..., :NA_WIN_W - 1]], axis=-1)
    tiled = jnp.tile(ring, (1, 1, GRID_W))[..., :GRID_W * (period - 1)]
    toeplitz = tiled.reshape(N_HEADS, n_dr, GRID_W, period - 1)[..., :GRID_W]
    c = jnp.arange(GRID_W)
    col_start = jnp.clip(c - NA_WIN_W // 2, 0, GRID_W - NA_WIN_W)
    kc = jnp.arange(GRID_W)
    valid = (kc[None, :] >= col_start[:, None]) & (kc[None, :] < col_start[:, None] + NA_WIN_W)
    toeplitz = jnp.where(valid[None, None], toeplitz, MASK_VALUE)
    t = jnp.stack([toeplitz[:, NA_WIN_H - 1 - d:2 * NA_WIN_H - 1 - d] for d in range(NA_WIN_H)])
    t = jnp.transpose(t, (0, 1, 3, 2, 4))
    return t.reshape(NA_WIN_H, N_HEADS, GRID_W, WIN_TOKENS)


def _mixer(x, k, v, g_pre, w_uq, w_gate, bias, w_pool, pool_scale, w_out, g_post):
    batch, seq, _ = x.shape

    def prev_halo(halo, width):
        per_tile = TOKEN_TILE // halo
        return pl.BlockSpec((None, halo, width),
                            lambda b, i: (b, jnp.maximum(i * per_tile - 1, 0), 0))

    def next_halo(halo, width):
        per_tile = TOKEN_TILE // halo
        last = seq // halo - 1
        return pl.BlockSpec((None, halo, width),
                            lambda b, i: (b, jnp.minimum((i + 1) * per_tile, last), 0))

    in_specs = [
        _tile(D_MODEL), prev_halo(POOL_HALO, D_MODEL), next_halo(POOL_HALO, D_MODEL),
        prev_halo(KV_HALO, D_ATT_QK), _tile(D_ATT_QK), next_halo(KV_HALO, D_ATT_QK),
        prev_halo(KV_HALO, D_MODEL), _tile(D_MODEL), next_halo(KV_HALO, D_MODEL),
        _resident((1, D_MODEL)),
        _resident((D_MODEL, D_POOL + D_ATT_QK)),
        _resident((D_MODEL, 2 * D_MODEL)),
        _resident((NA_WIN_H, N_HEADS, GRID_W, WIN_TOKENS)),
        _resident((POOL_GROUPS, POOL_GROUP_C, POOL_OUT_C)),
        _resident((1, D_MODEL)),
        _resident((D_MODEL, D_MODEL)),
        _resident((1, D_MODEL)),
    ]
    n_slots = ATTN_ROWS_PER_ITER * N_HEADS
    return pl.pallas_call(
        functools.partial(_mixer_kernel, seq=seq),
        grid=(batch, seq // TOKEN_TILE),
        in_specs=in_specs,
        out_specs=_tile(D_MODEL),
        out_shape=jax.ShapeDtypeStruct((batch, seq, D_MODEL), jnp.float32),
        scratch_shapes=[
            pltpu.VMEM((TOKEN_TILE + 2 * POOL_HALO, D_MODEL), jnp.bfloat16),
            pltpu.VMEM((TOKEN_TILE + 2 * POOL_HALO, D_POOL), jnp.float32),
            pltpu.VMEM((TOKEN_TILE, D_ATT_QK), jnp.bfloat16),
            pltpu.VMEM((TOKEN_TILE, D_MODEL), jnp.float32),
            pltpu.VMEM((TOKEN_TILE, D_MODEL), jnp.float32),
            pltpu.VMEM((TOKEN_TILE, D_MODEL), jnp.float32),
            pltpu.VMEM((n_slots, GRID_W, WIN_TOKENS), jnp.float32),
            pltpu.VMEM((n_slots, GRID_W, WIN_TOKENS), jnp.bfloat16),
            pltpu.VMEM((KV_WINDOW, D_ATT_QK), jnp.bfloat16),
            pltpu.VMEM((KV_WINDOW, D_MODEL), jnp.bfloat16),
        ],
        compiler_params=_params(),
        name="mixer",
    )(x, x, x, k, k, k, v, v, v, g_pre, w_uq, w_gate, bias, w_pool, pool_scale, w_out, g_post)


def _ffn_kernel(x_ref, g_pre_ref, w_gu_ref, w_down_ref, g_post_ref, o_ref, act_ref):
    for r0 in range(0, FFN_TILE, FFN_SUBTILE):
        rows = slice(r0, r0 + FFN_SUBTILE)
        x = x_ref[rows, :]
        h = _rmsnorm(x, g_pre_ref[...]).astype(jnp.bfloat16)
        for c in range(0, D_FF, FF_CHUNK):
            gate = jnp.dot(h, w_gu_ref[:, c:c + FF_CHUNK], preferred_element_type=jnp.float32)
            up = jnp.dot(h, w_gu_ref[:, D_FF + c:D_FF + c + FF_CHUNK],
                         preferred_element_type=jnp.float32)
            act_ref[rows, c:c + FF_CHUNK] = (jax.nn.silu(gate) * up).astype(jnp.bfloat16)
        f = jnp.dot(act_ref[rows, :], w_down_ref[...], preferred_element_type=jnp.float32)
        o_ref[rows, :] = x + _rmsnorm(f, g_post_ref[...])


def _ffn(x, g_pre, w_gu, w_down, g_post):
    batch, seq, _ = x.shape
    return pl.pallas_call(
        _ffn_kernel,
        grid=(batch, seq // FFN_TILE),
        in_specs=[_tile(D_MODEL, FFN_TILE), _resident((1, D_MODEL)),
                  _resident((D_MODEL, 2 * D_FF)), _resident((D_FF, D_MODEL)),
                  _resident((1, D_MODEL))],
        out_specs=_tile(D_MODEL, FFN_TILE),
        out_shape=jax.ShapeDtypeStruct((batch, seq, D_MODEL), jnp.float32),
        scratch_shapes=[pltpu.VMEM((FFN_TILE, D_FF), jnp.bfloat16)],
        compiler_params=_params(),
        name="ffn",
    )(x, g_pre, w_gu, w_down, g_post)


def _layer(x, p):
    seq = x.shape[1]
    assert seq % TOKEN_TILE == 0 and seq % KV_PROJ_TILE == 0 and seq % FFN_TILE == 0
    assert seq >= KV_WINDOW and seq // GRID_W >= NA_WIN_H
    k, v = _kv_proj(x, p["g_mix_pre"], p["w_kv"])
    x = _mixer(x, k, v, p["g_mix_pre"], p["w_uq"], p["w_gate"], p["bias"], p["w_pool"],
               p["pool_scale"], p["w_out"], p["g_mix_post"])
    return _ffn(x, p["g_ffn_pre"], p["w_gu"], p["w_down"], p["g_ffn_post"])


def kernel(x_prompt, x_sample, norm_mix_pre, w_in, w_pool_grp, pool_scale, attn_rpb, w_out,
           norm_mix_post, norm_ffn_pre, w_gate_up, w_down, norm_ffn_post):
    depth = w_in.shape[0]
    bf16 = jnp.bfloat16
    layers = []
    for l in range(depth):
        w_in_l = w_in[l].astype(bf16)
        layers.append(dict(
            g_mix_pre=norm_mix_pre[l][None, :],
            w_uq=w_in_l[:, :O_K],
            w_kv=w_in_l[:, O_K:O_GP],
            w_gate=w_in_l[:, O_GP:],
            w_pool=w_pool_grp[l].astype(bf16),
            pool_scale=pool_scale[l][None, :],
            bias=_bias_table(attn_rpb[l]),
            w_out=w_out[l].astype(bf16),
            g_mix_post=norm_mix_post[l][None, :],
            g_ffn_pre=norm_ffn_pre[l][None, :],
            w_gu=w_gate_up[l].astype(bf16),
            w_down=w_down[l].astype(bf16),
            g_ffn_post=norm_ffn_post[l][None, :],
        ))

    def trunk(x):
        for p in layers:
            x = _layer(x, p)
        return x

    return (trunk(x_prompt), trunk(x_sample))
```

```python
import functools

import jax
import jax.numpy as jnp
from jax import lax
from jax.experimental import pallas as pl
from jax.experimental.pallas import tpu as pltpu

D_MODEL = 1024
GRID_W = 64
POOL_WINDOWS = (2, 4, 8, 16)
POOL_GROUPS = 4
D_POOL = D_MODEL // 2
POOL_GROUP_C = D_POOL // POOL_GROUPS
POOL_OUT_C = D_MODEL // POOL_GROUPS
N_HEADS = D_MODEL // 128
D_ATT_QK = D_MODEL // 2
HEAD_DIM_QK = D_ATT_QK // N_HEADS
HEAD_DIM_V = D_MODEL // N_HEADS
NA_WIN_H = 8
NA_WIN_W = 16
D_IN = D_POOL + 2 * D_ATT_QK + D_MODEL + 2 * D_MODEL
D_FF = ((8 * D_MODEL + 3 * 256 - 1) // (3 * 256)) * 256
EPS = 1e-6

O_Q = D_POOL
O_K = O_Q + D_ATT_QK
O_V = O_K + D_ATT_QK
O_GP = O_V + D_MODEL
O_GA = O_GP + D_MODEL

QK_SCALE = HEAD_DIM_QK ** -0.5
MASK_VALUE = -1e30

TOKEN_TILE = 512
KV_PROJ_TILE = 1024
FFN_TILE = 1024
FFN_SUBTILE = 512
ROWS_PER_TILE = TOKEN_TILE // GRID_W
KV_HALO = (NA_WIN_H // 2) * GRID_W
KV_WINDOW = KV_HALO + TOKEN_TILE + KV_HALO
WIN_TOKENS = NA_WIN_H * GRID_W
POOL_HALO = 16
ATTN_ROWS_PER_ITER = 4
FF_CHUNK = 256
VMEM_LIMIT_BYTES = 58 * 1024 * 1024


def _rmsnorm(x, g):
    y = x * lax.rsqrt(jnp.mean(x * x, axis=-1, keepdims=True) + EPS)
    return y * g


def _resident(shape):
    zeros = (0,) * len(shape)
    return pl.BlockSpec(shape, lambda b, i: zeros, pipeline_mode=pl.Buffered(1))


def _tile(width, tokens=TOKEN_TILE):
    return pl.BlockSpec((None, tokens, width), lambda b, i: (b, i, 0))


def _params():
    return pltpu.CompilerParams(dimension_semantics=("arbitrary", "arbitrary"),
                                vmem_limit_bytes=VMEM_LIMIT_BYTES)


def _kv_proj_kernel(x_ref, g_ref, w_ref, k_ref, v_ref):
    h = _rmsnorm(x_ref[...], g_ref[...]).astype(jnp.bfloat16)
    for out_ref, col0 in ((k_ref, 0), (v_ref, D_ATT_QK)):
        for c in range(0, out_ref.shape[0] * 128, 512):
            y = jnp.dot(h, w_ref[:, col0 + c:col0 + c + 512],
                        preferred_element_type=jnp.float32).astype(jnp.bfloat16)
            for j in range(512 // 128):
                out_ref[c // 128 + j] = y[:, j * 128:(j + 1) * 128]


def _kv_proj(x, g_pre, w_kv):
    batch, seq, _ = x.shape
    return pl.pallas_call(
        _kv_proj_kernel,
        grid=(batch, seq // KV_PROJ_TILE),
        in_specs=[_tile(D_MODEL, KV_PROJ_TILE), _resident((1, D_MODEL)),
                  _resident((D_MODEL, D_ATT_QK + D_MODEL))],
        out_specs=tuple(
            pl.BlockSpec((None, width // 128, KV_PROJ_TILE, 128), lambda b, i: (b, 0, i, 0))
            for width in (D_ATT_QK, D_MODEL)),
        out_shape=tuple(
            jax.ShapeDtypeStruct((batch, width // 128, seq, 128), jnp.bfloat16)
            for width in (D_ATT_QK, D_MODEL)),
        compiler_params=_params(),
        name="kv_proj",
    )(x, g_pre, w_kv)


def _mixer_kernel(x_ref, x_prev_ref, x_next_ref, k_prev_ref, k_cur_ref, k_next_ref,
                  v_prev_ref, v_cur_ref, v_next_ref, g_pre_ref, w_uq_ref, w_gate_ref,
                  bias_ref, w_pool_ref, pool_scale_ref, w_out_ref, g_post_ref,
                  o_ref, h_ref, upad_ref, q_ref, ga_ref, mix_ref, attn_ref, score_ref, prob_ref,
                  k_ref, v_ref, *, seq):
    i = pl.program_id(1)
    rows = seq // GRID_W
    tile_rows = slice(POOL_HALO, POOL_HALO + TOKEN_TILE)

    g_pre = g_pre_ref[...]
    h_ref[0:POOL_HALO, :] = _rmsnorm(x_prev_ref[...], g_pre).astype(jnp.bfloat16)
    h_ref[tile_rows, :] = _rmsnorm(x_ref[...], g_pre).astype(jnp.bfloat16)
    h_ref[POOL_HALO + TOKEN_TILE:, :] = _rmsnorm(x_next_ref[...], g_pre).astype(jnp.bfloat16)

    u_ext = jnp.dot(h_ref[...], w_uq_ref[:, :D_POOL], preferred_element_type=jnp.float32)
    tok = (i * TOKEN_TILE - POOL_HALO
           + lax.broadcasted_iota(jnp.int32, (TOKEN_TILE + 2 * POOL_HALO, POOL_GROUP_C), 0))
    in_seq = (tok >= 0) & (tok < seq)
    for g in range(POOL_GROUPS):
        upad_ref[g] = jnp.where(in_seq, u_ext[:, g * POOL_GROUP_C:(g + 1) * POOL_GROUP_C], 0.0)
    h = h_ref[tile_rows, :]
    q = (jnp.dot(h, w_uq_ref[:, D_POOL:], preferred_element_type=jnp.float32)
         * QK_SCALE).astype(jnp.bfloat16)
    for grp in range(N_HEADS // 2):
        q_ref[grp] = q[:, grp * 128:(grp + 1) * 128]
    for c in range(0, D_MODEL, 512):
        ga_ref[:, c:c + 512] = jax.nn.sigmoid(
            jnp.dot(h, w_gate_ref[:, D_MODEL + c:D_MODEL + c + 512],
                    preferred_element_type=jnp.float32))

    t = i * TOKEN_TILE + lax.broadcasted_iota(jnp.int32, (TOKEN_TILE, POOL_GROUP_C), 0)
    for g, w in enumerate(POOL_WINDOWS):
        half = w // 2
        win_sum = upad_ref[g, POOL_HALO - half:POOL_HALO - half + TOKEN_TILE, :]
        for d in range(-half + 1, half):
            win_sum = win_sum + upad_ref[g, POOL_HALO + d:POOL_HALO + d + TOKEN_TILE, :]
        cnt = jnp.minimum(t + half, seq) - jnp.maximum(t - half, 0)
        pooled = win_sum / cnt.astype(jnp.float32) - upad_ref[g, tile_rows, :]
        y = jnp.dot(pooled.astype(jnp.bfloat16), w_pool_ref[g],
                    preferred_element_type=jnp.float32)
        cols = slice(g * POOL_OUT_C, (g + 1) * POOL_OUT_C)
        gate = jax.nn.sigmoid(jnp.dot(h, w_gate_ref[:, cols], preferred_element_type=jnp.float32))
        mix_ref[:, cols] = gate * (y * pool_scale_ref[:, cols])

    lane = lax.broadcasted_iota(jnp.int32, (GRID_W, 128), 1)
    pieces = ((0, KV_HALO), (KV_HALO, TOKEN_TILE), (KV_HALO + TOKEN_TILE, KV_HALO))
    for srcs, dst in (((k_prev_ref, k_cur_ref, k_next_ref), k_ref),
                      ((v_prev_ref, v_cur_ref, v_next_ref), v_ref)):
        for src, (row0, n_rows) in zip(srcs, pieces):
            dst[:, row0:row0 + n_rows, :] = src[...]
    first_row_in_window = i * ROWS_PER_TILE - KV_HALO // GRID_W

    def rows_body(it, carry):
        items = []
        for sub in range(ATTN_ROWS_PER_ITER):
            rr = it * ATTN_ROWS_PER_ITER + sub
            r = i * ROWS_PER_TILE + rr
            row_start = jnp.clip(r - NA_WIN_H // 2, 0, rows - NA_WIN_H)
            delta = r - row_start
            start = pl.multiple_of((row_start - first_row_in_window) * GRID_W, GRID_W)
            tok_q = pl.multiple_of(rr * GRID_W, GRID_W)
            for hd in range(N_HEADS):
                items.append((sub * N_HEADS + hd, hd, delta, start, tok_q))

        for slot, hd, delta, start, tok_q in items:
            in_head = (lane >= (hd % 2) * HEAD_DIM_QK) & (lane < (hd % 2 + 1) * HEAD_DIM_QK)
            q_h = jnp.where(in_head, q_ref[hd // 2, pl.ds(tok_q, GRID_W), :],
                            jnp.zeros((), jnp.bfloat16))
            k_h = k_ref[hd // 2, pl.ds(start, WIN_TOKENS), :]
            s = lax.dot_general(q_h, k_h, (((1,), (1,)), ((), ())),
                                preferred_element_type=jnp.float32)
            score_ref[slot] = s + bias_ref[delta, hd]
        for slot, hd, delta, start, tok_q in items:
            s = score_ref[slot]
            e = jnp.exp(s - jnp.max(s, axis=-1, keepdims=True))
            p = e / jnp.sum(e, axis=-1, keepdims=True)
            prob_ref[slot] = p.astype(jnp.bfloat16)
        for slot, hd, delta, start, tok_q in items:
            attn_ref[hd, pl.ds(tok_q, GRID_W), :] = jnp.dot(
                prob_ref[slot], v_ref[hd, pl.ds(start, WIN_TOKENS), :],
                preferred_element_type=jnp.float32)
        return carry

    lax.fori_loop(0, ROWS_PER_TILE // ATTN_ROWS_PER_ITER, rows_body, 0)

    attn = jnp.concatenate([attn_ref[hd] for hd in range(N_HEADS)], axis=1)
    mixed = mix_ref[...] + ga_ref[...] * attn
    y = jnp.dot(mixed.astype(jnp.bfloat16), w_out_ref[...],
                preferred_element_type=jnp.float32)
    o_ref[...] = x_ref[...] + _rmsnorm(y, g_post_ref[...])


def _bias_table(rpb):
    n_dr = 2 * NA_WIN_H - 1
    period = 2 * GRID_W - 1
    rpb = rpb.astype(jnp.float32)
    ring = jnp.concatenate(
        [rpb[..., NA_WIN_W - 1:],
         jnp.full((N_HEADS, n_dr, period - (2 * NA_WIN_W - 1)), MASK_VALUE, jnp.float32),
         rpb[..., :NA_WIN_W - 1]], axis=-1)
    tiled = jnp.tile(ring, (1, 1, GRID_W))[..., :GRID_W * (period - 1)]
    toeplitz = tiled.reshape(N_HEADS, n_dr, GRID_W, period - 1)[..., :GRID_W]
    c = jnp.arange(GRID_W)
    col_start = jnp.clip(c - NA_WIN_W // 2, 0, GRID_W - NA_WIN_W)
    kc = jnp.arange(GRID_W)
    valid = (kc[None, :] >= col_start[:, None]) & (kc[None, :] < col_start[:, None] + NA_WIN_W)
    toeplitz = jnp.where(valid[None, None], toeplitz, MASK_VALUE)
    t = jnp.stack([toeplitz[:, NA_WIN_H - 1 - d:2 * NA_WIN_H - 1 - d] for d in range(NA_WIN_H)])
    t = jnp.transpose(t, (0, 1, 3, 2, 4))
    return t.reshape(NA_WIN_H, N_HEADS, GRID_W, WIN_TOKENS)


def _mixer(x, k, v, g_pre, w_uq, w_gate, bias, w_pool, pool_scale, w_out, g_post):
    batch, seq, _ = x.shape

    def prev_halo(halo, width):
        per_tile = TOKEN_TILE // halo
        return pl.BlockSpec((None, halo, width),
                            lambda b, i: (b, jnp.maximum(i * per_tile - 1, 0), 0))

    def next_halo(halo, width):
        per_tile = TOKEN_TILE // halo
        last = seq // halo - 1
        return pl.BlockSpec((None, halo, width),
                            lambda b, i: (b, jnp.minimum((i + 1) * per_tile, last), 0))

    def kv_pieces(width):
        grps = width // 128
        per_tile = TOKEN_TILE // KV_HALO
        last = seq // KV_HALO - 1
        return [
            pl.BlockSpec((None, grps, KV_HALO, 128),
                         lambda b, i: (b, 0, jnp.maximum(i * per_tile - 1, 0), 0)),
            pl.BlockSpec((None, grps, TOKEN_TILE, 128), lambda b, i: (b, 0, i, 0)),
            pl.BlockSpec((None, grps, KV_HALO, 128),
                         lambda b, i: (b, 0, jnp.minimum((i + 1) * per_tile, last), 0)),
        ]

    in_specs = [
        _tile(D_MODEL), prev_halo(POOL_HALO, D_MODEL), next_halo(POOL_HALO, D_MODEL),
        *kv_pieces(D_ATT_QK), *kv_pieces(D_MODEL),
        _resident((1, D_MODEL)),
        _resident((D_MODEL, D_POOL + D_ATT_QK)),
        _resident((D_MODEL, 2 * D_MODEL)),
        _resident((NA_WIN_H, N_HEADS, GRID_W, WIN_TOKENS)),
        _resident((POOL_GROUPS, POOL_GROUP_C, POOL_OUT_C)),
        _resident((1, D_MODEL)),
        _resident((D_MODEL, D_MODEL)),
        _resident((1, D_MODEL)),
    ]
    n_slots = ATTN_ROWS_PER_ITER * N_HEADS
    return pl.pallas_call(
        functools.partial(_mixer_kernel, seq=seq),
        grid=(batch, seq // TOKEN_TILE),
        in_specs=in_specs,
        out_specs=_tile(D_MODEL),
        out_shape=jax.ShapeDtypeStruct((batch, seq, D_MODEL), jnp.float32),
        scratch_shapes=[
            pltpu.VMEM((TOKEN_TILE + 2 * POOL_HALO, D_MODEL), jnp.bfloat16),
            pltpu.VMEM((POOL_GROUPS, TOKEN_TILE + 2 * POOL_HALO, POOL_GROUP_C),
                       jnp.float32),
            pltpu.VMEM((N_HEADS // 2, TOKEN_TILE, 128), jnp.bfloat16),
            pltpu.VMEM((TOKEN_TILE, D_MODEL), jnp.float32),
            pltpu.VMEM((TOKEN_TILE, D_MODEL), jnp.float32),
            pltpu.VMEM((N_HEADS, TOKEN_TILE, HEAD_DIM_V), jnp.float32),
            pltpu.VMEM((n_slots, GRID_W, WIN_TOKENS), jnp.float32),
            pltpu.VMEM((n_slots, GRID_W, WIN_TOKENS), jnp.bfloat16),
            pltpu.VMEM((N_HEADS // 2, KV_WINDOW, 128), jnp.bfloat16),
            pltpu.VMEM((N_HEADS, KV_WINDOW, HEAD_DIM_V), jnp.bfloat16),
        ],
        compiler_params=_params(),
        name="mixer",
    )(x, x, x, k, k, k, v, v, v, g_pre, w_uq, w_gate, bias, w_pool, pool_scale, w_out, g_post)


def _ffn_kernel(x_ref, g_pre_ref, w_gu_ref, w_down_ref, g_post_ref, o_ref, act_ref):
    for r0 in range(0, FFN_TILE, FFN_SUBTILE):
        rows = slice(r0, r0 + FFN_SUBTILE)
        x = x_ref[rows, :]
        h = _rmsnorm(x, g_pre_ref[...]).astype(jnp.bfloat16)
        for c in range(0, D_FF, FF_CHUNK):
            gate = jnp.dot(h, w_gu_ref[:, c:c + FF_CHUNK], preferred_element_type=jnp.float32)
            up = jnp.dot(h, w_gu_ref[:, D_FF + c:D_FF + c + FF_CHUNK],
                         preferred_element_type=jnp.float32)
            act_ref[rows, c:c + FF_CHUNK] = (jax.nn.silu(gate) * up).astype(jnp.bfloat16)
        f = jnp.dot(act_ref[rows, :], w_down_ref[...], preferred_element_type=jnp.float32)
        o_ref[rows, :] = x + _rmsnorm(f, g_post_ref[...])


def _ffn(x, g_pre, w_gu, w_down, g_post):
    batch, seq, _ = x.shape
    return pl.pallas_call(
        _ffn_kernel,
        grid=(batch, seq // FFN_TILE),
        in_specs=[_tile(D_MODEL, FFN_TILE), _resident((1, D_MODEL)),
                  _resident((D_MODEL, 2 * D_FF)), _resident((D_FF, D_MODEL)),
                  _resident((1, D_MODEL))],
        out_specs=_tile(D_MODEL, FFN_TILE),
        out_shape=jax.ShapeDtypeStruct((batch, seq, D_MODEL), jnp.float32),
        scratch_shapes=[pltpu.VMEM((FFN_TILE, D_FF), jnp.bfloat16)],
        compiler_params=_params(),
        name="ffn",
    )(x, g_pre, w_gu, w_down, g_post)


def _layer(x, p):
    seq = x.shape[1]
    assert seq % TOKEN_TILE == 0 and seq % KV_PROJ_TILE == 0 and seq % FFN_TILE == 0
    assert seq >= KV_WINDOW and seq // GRID_W >= NA_WIN_H
    k, v = _kv_proj(x, p["g_mix_pre"], p["w_kv"])
    x = _mixer(x, k, v, p["g_mix_pre"], p["w_uq"], p["w_gate"], p["bias"], p["w_pool"],
               p["pool_scale"], p["w_out"], p["g_mix_post"])
    return _ffn(x, p["g_ffn_pre"], p["w_gu"], p["w_down"], p["g_ffn_post"])


def kernel(x_prompt, x_sample, norm_mix_pre, w_in, w_pool_grp, pool_scale, attn_rpb, w_out,
           norm_mix_post, norm_ffn_pre, w_gate_up, w_down, norm_ffn_post):
    depth = w_in.shape[0]
    bf16 = jnp.bfloat16
    layers = []
    for l in range(depth):
        w_in_l = w_in[l].astype(bf16)
        layers.append(dict(
            g_mix_pre=norm_mix_pre[l][None, :],
            w_uq=w_in_l[:, :O_K],
            w_kv=w_in_l[:, O_K:O_GP],
            w_gate=w_in_l[:, O_GP:],
            w_pool=w_pool_grp[l].astype(bf16),
            pool_scale=pool_scale[l][None, :],
            bias=_bias_table(attn_rpb[l]),
            w_out=w_out[l].astype(bf16),
            g_mix_post=norm_mix_post[l][None, :],
            g_ffn_pre=norm_ffn_pre[l][None, :],
            w_gu=w_gate_up[l].astype(bf16),
            w_down=w_down[l].astype(bf16),
            g_ffn_post=norm_ffn_post[l][None, :],
        ))

    def trunk(x):
        for p in layers:
            x = _layer(x, p)
        return x

    return (trunk(x_prompt), trunk(x_sample))
```

```python
import functools

import jax
import jax.numpy as jnp
from jax import lax
from jax.experimental import pallas as pl
from jax.experimental.pallas import tpu as pltpu

D_MODEL = 1024
GRID_W = 64
POOL_WINDOWS = (2, 4, 8, 16)
POOL_GROUPS = 4
D_POOL = D_MODEL // 2
POOL_GROUP_C = D_POOL // POOL_GROUPS
POOL_OUT_C = D_MODEL // POOL_GROUPS
N_HEADS = D_MODEL // 128
D_ATT_QK = D_MODEL // 2
HEAD_DIM_QK = D_ATT_QK // N_HEADS
HEAD_DIM_V = D_MODEL // N_HEADS
NA_WIN_H = 8
NA_WIN_W = 16
D_IN = D_POOL + 2 * D_ATT_QK + D_MODEL + 2 * D_MODEL
D_FF = ((8 * D_MODEL + 3 * 256 - 1) // (3 * 256)) * 256
EPS = 1e-6

O_Q = D_POOL
O_K = O_Q + D_ATT_QK
O_V = O_K + D_ATT_QK
O_GP = O_V + D_MODEL
O_GA = O_GP + D_MODEL

QK_SCALE = HEAD_DIM_QK ** -0.5
MASK_VALUE = -1e30

TOKEN_TILE = 512
KV_PROJ_TILE = 1024
FFN_TILE = 1024
FFN_SUBTILE = 512
ROWS_PER_TILE = TOKEN_TILE // GRID_W
KV_HALO = (NA_WIN_H // 2) * GRID_W
KV_WINDOW = KV_HALO + TOKEN_TILE + KV_HALO
WIN_TOKENS = NA_WIN_H * GRID_W
POOL_HALO = 16
FF_CHUNK = 256
VMEM_LIMIT_BYTES = 58 * 1024 * 1024


def _rmsnorm(x, g):
    y = x * lax.rsqrt(jnp.mean(x * x, axis=-1, keepdims=True) + EPS)
    return y * g


def _resident(shape):
    zeros = (0,) * len(shape)
    return pl.BlockSpec(shape, lambda b, i: zeros, pipeline_mode=pl.Buffered(1))


def _tile(width, tokens=TOKEN_TILE):
    return pl.BlockSpec((None, tokens, width), lambda b, i: (b, i, 0))


def _params():
    return pltpu.CompilerParams(dimension_semantics=("arbitrary", "arbitrary"),
                                vmem_limit_bytes=VMEM_LIMIT_BYTES)


def _kv_proj_kernel(x_ref, g_ref, w_ref, k_ref, v_ref):
    h = _rmsnorm(x_ref[...], g_ref[...]).astype(jnp.bfloat16)
    for out_ref, col0 in ((k_ref, 0), (v_ref, D_ATT_QK)):
        for c in range(0, out_ref.shape[0] * 128, 512):
            y = jnp.dot(h, w_ref[:, col0 + c:col0 + c + 512],
                        preferred_element_type=jnp.float32).astype(jnp.bfloat16)
            for j in range(512 // 128):
                out_ref[c // 128 + j] = y[:, j * 128:(j + 1) * 128]


def _kv_proj(x, g_pre, w_kv):
    batch, seq, _ = x.shape
    return pl.pallas_call(
        _kv_proj_kernel,
        grid=(batch, seq // KV_PROJ_TILE),
        in_specs=[_tile(D_MODEL, KV_PROJ_TILE), _resident((1, D_MODEL)),
                  _resident((D_MODEL, D_ATT_QK + D_MODEL))],
        out_specs=tuple(
            pl.BlockSpec((None, width // 128, KV_PROJ_TILE, 128), lambda b, i: (b, 0, i, 0))
            for width in (D_ATT_QK, D_MODEL)),
        out_shape=tuple(
            jax.ShapeDtypeStruct((batch, width // 128, seq, 128), jnp.bfloat16)
            for width in (D_ATT_QK, D_MODEL)),
        compiler_params=_params(),
        name="kv_proj",
    )(x, g_pre, w_kv)


def _mixer_kernel(x_ref, x_prev_ref, x_next_ref, k_prev_ref, k_cur_ref, k_next_ref,
                  v_prev_ref, v_cur_ref, v_next_ref, g_pre_ref, w_uq_ref, w_gate_ref,
                  bias_ref, w_pool_ref, pool_scale_ref, w_out_ref, g_post_ref,
                  o_ref, h_ref, upad_ref, q_ref, ga_ref, mix_ref, attn_ref, score_ref, prob_ref,
                  k_ref, v_ref, *, seq):
    i = pl.program_id(1)
    rows = seq // GRID_W
    tile_rows = slice(POOL_HALO, POOL_HALO + TOKEN_TILE)

    g_pre = g_pre_ref[...]
    h_ref[0:POOL_HALO, :] = _rmsnorm(x_prev_ref[...], g_pre).astype(jnp.bfloat16)
    h_ref[tile_rows, :] = _rmsnorm(x_ref[...], g_pre).astype(jnp.bfloat16)
    h_ref[POOL_HALO + TOKEN_TILE:, :] = _rmsnorm(x_next_ref[...], g_pre).astype(jnp.bfloat16)

    h = h_ref[tile_rows, :]
    q = (jnp.dot(h, w_uq_ref[:, D_POOL:], preferred_element_type=jnp.float32)
         * QK_SCALE).astype(jnp.bfloat16)
    for grp in range(N_HEADS // 2):
        q_ref[grp] = q[:, grp * 128:(grp + 1) * 128]

    lane = lax.broadcasted_iota(jnp.int32, (GRID_W, 128), 1)
    pieces = ((0, KV_HALO), (KV_HALO, TOKEN_TILE), (KV_HALO + TOKEN_TILE, KV_HALO))
    for srcs, dst in (((k_prev_ref, k_cur_ref, k_next_ref), k_ref),
                      ((v_prev_ref, v_cur_ref, v_next_ref), v_ref)):
        for src, (row0, n_rows) in zip(srcs, pieces):
            dst[:, row0:row0 + n_rows, :] = src[...]
    first_row_in_window = i * ROWS_PER_TILE - KV_HALO // GRID_W

    items = []
    for rr in range(ROWS_PER_TILE):
        r = i * ROWS_PER_TILE + rr
        row_start = jnp.clip(r - NA_WIN_H // 2, 0, rows - NA_WIN_H)
        delta = r - row_start
        start = pl.multiple_of((row_start - first_row_in_window) * GRID_W, GRID_W)
        for hd in range(N_HEADS):
            items.append((rr * N_HEADS + hd, hd, delta, start, rr * GRID_W))

    for slot, hd, delta, start, tok_q in items:
        in_head = (lane >= (hd % 2) * HEAD_DIM_QK) & (lane < (hd % 2 + 1) * HEAD_DIM_QK)
        q_h = jnp.where(in_head, q_ref[hd // 2, tok_q:tok_q + GRID_W, :],
                        jnp.zeros((), jnp.bfloat16))
        k_h = k_ref[hd // 2, pl.ds(start, WIN_TOKENS), :]
        s = lax.dot_general(q_h, k_h, (((1,), (1,)), ((), ())),
                            preferred_element_type=jnp.float32)
        score_ref[slot] = s + bias_ref[delta, hd]

    u_ext = jnp.dot(h_ref[...], w_uq_ref[:, :D_POOL], preferred_element_type=jnp.float32)
    tok = (i * TOKEN_TILE - POOL_HALO
           + lax.broadcasted_iota(jnp.int32, (TOKEN_TILE + 2 * POOL_HALO, POOL_GROUP_C), 0))
    in_seq = (tok >= 0) & (tok < seq)
    for g in range(POOL_GROUPS):
        upad_ref[g] = jnp.where(in_seq, u_ext[:, g * POOL_GROUP_C:(g + 1) * POOL_GROUP_C], 0.0)
    for c in range(0, D_MODEL, 512):
        ga_ref[:, c:c + 512] = jax.nn.sigmoid(
            jnp.dot(h, w_gate_ref[:, D_MODEL + c:D_MODEL + c + 512],
                    preferred_element_type=jnp.float32))

    t = i * TOKEN_TILE + lax.broadcasted_iota(jnp.int32, (TOKEN_TILE, POOL_GROUP_C), 0)
    for g, w in enumerate(POOL_WINDOWS):
        half = w // 2
        win_sum = upad_ref[g, POOL_HALO - half:POOL_HALO - half + TOKEN_TILE, :]
        for d in range(-half + 1, half):
            win_sum = win_sum + upad_ref[g, POOL_HALO + d:POOL_HALO + d + TOKEN_TILE, :]
        cnt = jnp.minimum(t + half, seq) - jnp.maximum(t - half, 0)
        pooled = win_sum / cnt.astype(jnp.float32) - upad_ref[g, tile_rows, :]
        y = jnp.dot(pooled.astype(jnp.bfloat16), w_pool_ref[g],
                    preferred_element_type=jnp.float32)
        cols = slice(g * POOL_OUT_C, (g + 1) * POOL_OUT_C)
        gate = jax.nn.sigmoid(jnp.dot(h, w_gate_ref[:, cols], preferred_element_type=jnp.float32))
        mix_ref[:, cols] = gate * (y * pool_scale_ref[:, cols])

    for slot, hd, delta, start, tok_q in items:
        s = score_ref[slot]
        e = jnp.exp(s - jnp.max(s, axis=-1, keepdims=True))
        p = e / jnp.sum(e, axis=-1, keepdims=True)
        prob_ref[slot] = p.astype(jnp.bfloat16)
    for slot, hd, delta, start, tok_q in items:
        attn_ref[hd, tok_q:tok_q + GRID_W, :] = jnp.dot(
            prob_ref[slot], v_ref[hd, pl.ds(start, WIN_TOKENS), :],
            preferred_element_type=jnp.float32)

    attn = jnp.concatenate([attn_ref[hd] for hd in range(N_HEADS)], axis=1)
    mixed = mix_ref[...] + ga_ref[...] * attn
    y = jnp.dot(mixed.astype(jnp.bfloat16), w_out_ref[...],
                preferred_element_type=jnp.float32)
    o_ref[...] = x_ref[...] + _rmsnorm(y, g_post_ref[...])


def _bias_table(rpb):
    n_dr = 2 * NA_WIN_H - 1
    period = 2 * GRID_W - 1
    rpb = rpb.astype(jnp.float32)
    ring = jnp.concatenate(
        [rpb[..., NA_WIN_W - 1:],
         jnp.full((N_HEADS, n_dr, period - (2 * NA_WIN_W - 1)), MASK_VALUE, jnp.float32),
         rpb[..., :NA_WIN_W - 1]], axis=-1)
    tiled = jnp.tile(ring, (1, 1, GRID_W))[..., :GRID_W * (period - 1)]
    toeplitz = tiled.reshape(N_HEADS, n_dr, GRID_W, period - 1)[..., :GRID_W]
    c = jnp.arange(GRID_W)
    col_start = jnp.clip(c - NA_WIN_W // 2, 0, GRID_W - NA_WIN_W)
    kc = jnp.arange(GRID_W)
    valid = (kc[None, :] >= col_start[:, None]) & (kc[None, :] < col_start[:, None] + NA_WIN_W)
    toeplitz = jnp.where(valid[None, None], toeplitz, MASK_VALUE)
    t = jnp.stack([toeplitz[:, NA_WIN_H - 1 - d:2 * NA_WIN_H - 1 - d] for d in range(NA_WIN_H)])
    t = jnp.transpose(t, (0, 1, 3, 2, 4))
    return t.reshape(NA_WIN_H, N_HEADS, GRID_W, WIN_TOKENS)


def _mixer(x, k, v, g_pre, w_uq, w_gate, bias, w_pool, pool_scale, w_out, g_post):
    batch, seq, _ = x.shape

    def prev_halo(halo, width):
        per_tile = TOKEN_TILE // halo
        return pl.BlockSpec((None, halo, width),
                            lambda b, i: (b, jnp.maximum(i * per_tile - 1, 0), 0))

    def next_halo(halo, width):
        per_tile = TOKEN_TILE // halo
        last = seq // halo - 1
        return pl.BlockSpec((None, halo, width),
                            lambda b, i: (b, jnp.minimum((i + 1) * per_tile, last), 0))

    def kv_pieces(width):
        grps = width // 128
        per_tile = TOKEN_TILE // KV_HALO
        last = seq // KV_HALO - 1
        return [
            pl.BlockSpec((None, grps, KV_HALO, 128),
                         lambda b, i: (b, 0, jnp.maximum(i * per_tile - 1, 0), 0)),
            pl.BlockSpec((None, grps, TOKEN_TILE, 128), lambda b, i: (b, 0, i, 0)),
            pl.BlockSpec((None, grps, KV_HALO, 128),
                         lambda b, i: (b, 0, jnp.minimum((i + 1) * per_tile, last), 0)),
        ]

    in_specs = [
        _tile(D_MODEL), prev_halo(POOL_HALO, D_MODEL), next_halo(POOL_HALO, D_MODEL),
        *kv_pieces(D_ATT_QK), *kv_pieces(D_MODEL),
        _resident((1, D_MODEL)),
        _resident((D_MODEL, D_POOL + D_ATT_QK)),
        _resident((D_MODEL, 2 * D_MODEL)),
        _resident((NA_WIN_H, N_HEADS, GRID_W, WIN_TOKENS)),
        _resident((POOL_GROUPS, POOL_GROUP_C, POOL_OUT_C)),
        _resident((1, D_MODEL)),
        _resident((D_MODEL, D_MODEL)),
        _resident((1, D_MODEL)),
    ]
    n_slots = ROWS_PER_TILE * N_HEADS
    return pl.pallas_call(
        functools.partial(_mixer_kernel, seq=seq),
        grid=(batch, seq // TOKEN_TILE),
        in_specs=in_specs,
        out_specs=_tile(D_MODEL),
        out_shape=jax.ShapeDtypeStruct((batch, seq, D_MODEL), jnp.float32),
        scratch_shapes=[
            pltpu.VMEM((TOKEN_TILE + 2 * POOL_HALO, D_MODEL), jnp.bfloat16),
            pltpu.VMEM((POOL_GROUPS, TOKEN_TILE + 2 * POOL_HALO, POOL_GROUP_C),
                       jnp.float32),
            pltpu.VMEM((N_HEADS // 2, TOKEN_TILE, 128), jnp.bfloat16),
            pltpu.VMEM((TOKEN_TILE, D_MODEL), jnp.float32),
            pltpu.VMEM((TOKEN_TILE, D_MODEL), jnp.float32),
            pltpu.VMEM((N_HEADS, TOKEN_TILE, HEAD_DIM_V), jnp.float32),
            pltpu.VMEM((n_slots, GRID_W, WIN_TOKENS), jnp.float32),
            pltpu.VMEM((n_slots, GRID_W, WIN_TOKENS), jnp.bfloat16),
            pltpu.VMEM((N_HEADS // 2, KV_WINDOW, 128), jnp.bfloat16),
            pltpu.VMEM((N_HEADS, KV_WINDOW, HEAD_DIM_V), jnp.bfloat16),
        ],
        compiler_params=_params(),
        name="mixer",
    )(x, x, x, k, k, k, v, v, v, g_pre, w_uq, w_gate, bias, w_pool, pool_scale, w_out, g_post)


def _ffn_kernel(x_ref, g_pre_ref, w_gu_ref, w_down_ref, g_post_ref, o_ref, act_ref):
    for r0 in range(0, FFN_TILE, FFN_SUBTILE):
        rows = slice(r0, r0 + FFN_SUBTILE)
        x = x_ref[rows, :]
        h = _rmsnorm(x, g_pre_ref[...]).astype(jnp.bfloat16)
        for c in range(0, D_FF, FF_CHUNK):
            gate = jnp.dot(h, w_gu_ref[:, c:c + FF_CHUNK], preferred_element_type=jnp.float32)
            up = jnp.dot(h, w_gu_ref[:, D_FF + c:D_FF + c + FF_CHUNK],
                         preferred_element_type=jnp.float32)
            act_ref[rows, c:c + FF_CHUNK] = (jax.nn.silu(gate) * up).astype(jnp.bfloat16)
        f = jnp.dot(act_ref[rows, :], w_down_ref[...], preferred_element_type=jnp.float32)
        o_ref[rows, :] = x + _rmsnorm(f, g_post_ref[...])


def _ffn(x, g_pre, w_gu, w_down, g_post):
    batch, seq, _ = x.shape
    return pl.pallas_call(
        _ffn_kernel,
        grid=(batch, seq // FFN_TILE),
        in_specs=[_tile(D_MODEL, FFN_TILE), _resident((1, D_MODEL)),
                  _resident((D_MODEL, 2 * D_FF)), _resident((D_FF, D_MODEL)),
                  _resident((1, D_MODEL))],
        out_specs=_tile(D_MODEL, FFN_TILE),
        out_shape=jax.ShapeDtypeStruct((batch, seq, D_MODEL), jnp.float32),
        scratch_shapes=[pltpu.VMEM((FFN_TILE, D_FF), jnp.bfloat16)],
        compiler_params=_params(),
        name="ffn",
    )(x, g_pre, w_gu, w_down, g_post)


def _layer(x, p):
    seq = x.shape[1]
    assert seq % TOKEN_TILE == 0 and seq % KV_PROJ_TILE == 0 and seq % FFN_TILE == 0
    assert seq >= KV_WINDOW and seq // GRID_W >= NA_WIN_H
    k, v = _kv_proj(x, p["g_mix_pre"], p["w_kv"])
    x = _mixer(x, k, v, p["g_mix_pre"], p["w_uq"], p["w_gate"], p["bias"], p["w_pool"],
               p["pool_scale"], p["w_out"], p["g_mix_post"])
    return _ffn(x, p["g_ffn_pre"], p["w_gu"], p["w_down"], p["g_ffn_post"])


def kernel(x_prompt, x_sample, norm_mix_pre, w_in, w_pool_grp, pool_scale, attn_rpb, w_out,
           norm_mix_post, norm_ffn_pre, w_gate_up, w_down, norm_ffn_post):
    depth = w_in.shape[0]
    bf16 = jnp.bfloat16
    layers = []
    for l in range(depth):
        w_in_l = w_in[l].astype(bf16)
        layers.append(dict(
            g_mix_pre=norm_mix_pre[l][None, :],
            w_uq=w_in_l[:, :O_K],
            w_kv=w_in_l[:, O_K:O_GP],
            w_gate=w_in_l[:, O_GP:],
            w_pool=w_pool_grp[l].astype(bf16),
            pool_scale=pool_scale[l][None, :],
            bias=_bias_table(attn_rpb[l]),
            w_out=w_out[l].astype(bf16),
            g_mix_post=norm_mix_post[l][None, :],
            g_ffn_pre=norm_ffn_pre[l][None, :],
            w_gu=w_gate_up[l].astype(bf16),
            w_down=w_down[l].astype(bf16),
            g_ffn_post=norm_ffn_post[l][None, :],
        ))

    def trunk(x):
        for p in layers:
            x = _layer(x, p)
        return x

    return (trunk(x_prompt), trunk(x_sample))
```

```python
import functools

import jax
import jax.numpy as jnp
from jax import lax
from jax.experimental import pallas as pl
from jax.experimental.pallas import tpu as pltpu

D_MODEL = 1024
GRID_W = 64
POOL_WINDOWS = (2, 4, 8, 16)
POOL_GROUPS = 4
D_POOL = D_MODEL // 2
POOL_GROUP_C = D_POOL // POOL_GROUPS
POOL_OUT_C = D_MODEL // POOL_GROUPS
N_HEADS = D_MODEL // 128
D_ATT_QK = D_MODEL // 2
HEAD_DIM_QK = D_ATT_QK // N_HEADS
HEAD_DIM_V = D_MODEL // N_HEADS
NA_WIN_H = 8
NA_WIN_W = 16
D_IN = D_POOL + 2 * D_ATT_QK + D_MODEL + 2 * D_MODEL
D_FF = ((8 * D_MODEL + 3 * 256 - 1) // (3 * 256)) * 256
EPS = 1e-6

O_Q = D_POOL
O_K = O_Q + D_ATT_QK
O_V = O_K + D_ATT_QK
O_GP = O_V + D_MODEL
O_GA = O_GP + D_MODEL

QK_SCALE = HEAD_DIM_QK ** -0.5
MASK_VALUE = -1e30

TOKEN_TILE = 512
KV_PROJ_TILE = 1024
FFN_TILE = 1024
FFN_SUBTILE = 512
ROWS_PER_TILE = TOKEN_TILE // GRID_W
KV_HALO = (NA_WIN_H // 2) * GRID_W
KV_WINDOW = KV_HALO + TOKEN_TILE + KV_HALO
WIN_TOKENS = NA_WIN_H * GRID_W
K_GROUPS = N_HEADS // 2
LANE_BLOCKS = KV_WINDOW // 128
POOL_HALO = 16
FF_CHUNK = 256
VMEM_LIMIT_BYTES = 58 * 1024 * 1024


def _rmsnorm(x, g):
    y = x * lax.rsqrt(jnp.mean(x * x, axis=-1, keepdims=True) + EPS)
    return y * g


def _resident(shape):
    zeros = (0,) * len(shape)
    return pl.BlockSpec(shape, lambda b, i: zeros, pipeline_mode=pl.Buffered(1))


def _tile(width, tokens=TOKEN_TILE):
    return pl.BlockSpec((None, tokens, width), lambda b, i: (b, i, 0))


def _params():
    return pltpu.CompilerParams(dimension_semantics=("arbitrary", "arbitrary"),
                                vmem_limit_bytes=VMEM_LIMIT_BYTES)


def _kv_proj_kernel(x_ref, g_ref, wk_t_ref, wv_ref, kt_ref, v_ref):
    h = _rmsnorm(x_ref[...], g_ref[...]).astype(jnp.bfloat16)
    k_t = lax.dot_general(wk_t_ref[...], h, (((1,), (1,)), ((), ())),
                          preferred_element_type=jnp.float32).astype(jnp.bfloat16)
    for grp in range(kt_ref.shape[0]):
        kt_ref[grp] = k_t[grp * 128:(grp + 1) * 128, :]
    for c in range(0, D_MODEL, 512):
        y = jnp.dot(h, wv_ref[:, c:c + 512],
                    preferred_element_type=jnp.float32).astype(jnp.bfloat16)
        for j in range(512 // 128):
            v_ref[c // 128 + j] = y[:, j * 128:(j + 1) * 128]


def _kv_proj(x, g_pre, wk_t, w_v):
    batch, seq, _ = x.shape
    return pl.pallas_call(
        _kv_proj_kernel,
        grid=(batch, seq // KV_PROJ_TILE),
        in_specs=[_tile(D_MODEL, KV_PROJ_TILE), _resident((1, D_MODEL)),
                  _resident((D_ATT_QK, D_MODEL)), _resident((D_MODEL, D_MODEL))],
        out_specs=(
            pl.BlockSpec((None, K_GROUPS, 128, KV_PROJ_TILE), lambda b, i: (b, 0, 0, i)),
            pl.BlockSpec((None, N_HEADS, KV_PROJ_TILE, 128), lambda b, i: (b, 0, i, 0))),
        out_shape=(
            jax.ShapeDtypeStruct((batch, K_GROUPS, 128, seq), jnp.bfloat16),
            jax.ShapeDtypeStruct((batch, N_HEADS, seq, 128), jnp.bfloat16)),
        compiler_params=_params(),
        name="kv_proj",
    )(x, g_pre, wk_t, w_v)


def _mixer_kernel(x_ref, x_prev_ref, x_next_ref, k_prev_ref, k_cur_ref, k_next_ref,
                  v_prev_ref, v_cur_ref, v_next_ref, g_pre_ref, w_uq_ref, w_gate_ref,
                  bias_ref, w_pool_ref, pool_scale_ref, w_out_ref, g_post_ref,
                  o_ref, h_ref, upad_ref, q_ref, ga_ref, mix_ref, attn_ref, score_ref, prob_ref,
                  kt_ref, v_ref, *, seq):
    i = pl.program_id(1)
    rows = seq // GRID_W
    tile_rows = slice(POOL_HALO, POOL_HALO + TOKEN_TILE)

    g_pre = g_pre_ref[...]
    h_ref[0:POOL_HALO, :] = _rmsnorm(x_prev_ref[...], g_pre).astype(jnp.bfloat16)
    h_ref[tile_rows, :] = _rmsnorm(x_ref[...], g_pre).astype(jnp.bfloat16)
    h_ref[POOL_HALO + TOKEN_TILE:, :] = _rmsnorm(x_next_ref[...], g_pre).astype(jnp.bfloat16)

    h = h_ref[tile_rows, :]
    q = (jnp.dot(h, w_uq_ref[:, D_POOL:], preferred_element_type=jnp.float32)
         * QK_SCALE).astype(jnp.bfloat16)
    for grp in range(N_HEADS // 2):
        q_ref[grp] = q[:, grp * 128:(grp + 1) * 128]

    lane = lax.broadcasted_iota(jnp.int32, (GRID_W, 128), 1)
    pieces = ((0, KV_HALO), (KV_HALO, TOKEN_TILE), (KV_HALO + TOKEN_TILE, KV_HALO))
    for src, (row0, n_rows) in zip((v_prev_ref, v_cur_ref, v_next_ref), pieces):
        v_ref[:, row0:row0 + n_rows, :] = src[...]
    for grp in range(K_GROUPS):
        k_t = jnp.concatenate([k_prev_ref[grp], k_cur_ref[grp], k_next_ref[grp]], axis=1)
        shifted = pltpu.bitcast(
            pltpu.roll(pltpu.bitcast(k_t, jnp.uint32), KV_WINDOW - GRID_W, axis=1), jnp.bfloat16)
        for blk in range(LANE_BLOCKS):
            lanes = slice(blk * 128, (blk + 1) * 128)
            kt_ref[0, grp, blk] = k_t[:, lanes]
            kt_ref[1, grp, blk] = shifted[:, lanes]
    first_row_in_window = i * ROWS_PER_TILE - KV_HALO // GRID_W

    items = []
    for rr in range(ROWS_PER_TILE):
        r = i * ROWS_PER_TILE + rr
        row_start = jnp.clip(r - NA_WIN_H // 2, 0, rows - NA_WIN_H)
        delta = r - row_start
        win_row = row_start - first_row_in_window
        start = pl.multiple_of(win_row * GRID_W, GRID_W)
        for hd in range(N_HEADS):
            items.append((rr * N_HEADS + hd, hd, delta, start, win_row, rr * GRID_W))

    for slot, hd, delta, start, win_row, tok_q in items:
        in_head = (lane >= (hd % 2) * HEAD_DIM_QK) & (lane < (hd % 2 + 1) * HEAD_DIM_QK)
        q_h = jnp.where(in_head, q_ref[hd // 2, tok_q:tok_q + GRID_W, :],
                        jnp.zeros((), jnp.bfloat16))
        k_t = jnp.concatenate(
            [kt_ref[win_row % 2, hd // 2, win_row // 2 + blk] for blk in range(WIN_TOKENS // 128)],
            axis=1)
        s = jnp.dot(q_h, k_t, preferred_element_type=jnp.float32)
        score_ref[slot] = s + bias_ref[delta, hd]

    u_ext = jnp.dot(h_ref[...], w_uq_ref[:, :D_POOL], preferred_element_type=jnp.float32)
    tok = (i * TOKEN_TILE - POOL_HALO
           + lax.broadcasted_iota(jnp.int32, (TOKEN_TILE + 2 * POOL_HALO, POOL_GROUP_C), 0))
    in_seq = (tok >= 0) & (tok < seq)
    for g in range(POOL_GROUPS):
        upad_ref[g] = jnp.where(in_seq, u_ext[:, g * POOL_GROUP_C:(g + 1) * POOL_GROUP_C], 0.0)
    for c in range(0, D_MODEL, 512):
        ga_ref[:, c:c + 512] = jax.nn.sigmoid(
            jnp.dot(h, w_gate_ref[:, D_MODEL + c:D_MODEL + c + 512],
                    preferred_element_type=jnp.float32))

    t = i * TOKEN_TILE + lax.broadcasted_iota(jnp.int32, (TOKEN_TILE, POOL_GROUP_C), 0)
    for g, w in enumerate(POOL_WINDOWS):
        half = w // 2
        win_sum = upad_ref[g, POOL_HALO - half:POOL_HALO - half + TOKEN_TILE, :]
        for d in range(-half + 1, half):
            win_sum = win_sum + upad_ref[g, POOL_HALO + d:POOL_HALO + d + TOKEN_TILE, :]
        cnt = jnp.minimum(t + half, seq) - jnp.maximum(t - half, 0)
        pooled = win_sum / cnt.astype(jnp.float32) - upad_ref[g, tile_rows, :]
        y = jnp.dot(pooled.astype(jnp.bfloat16), w_pool_ref[g],
                    preferred_element_type=jnp.float32)
        cols = slice(g * POOL_OUT_C, (g + 1) * POOL_OUT_C)
        gate = jax.nn.sigmoid(jnp.dot(h, w_gate_ref[:, cols], preferred_element_type=jnp.float32))
        mix_ref[:, cols] = gate * (y * pool_scale_ref[:, cols])

    for slot, hd, delta, start, win_row, tok_q in items:
        s = score_ref[slot]
        e = jnp.exp(s - jnp.max(s, axis=-1, keepdims=True))
        p = e / jnp.sum(e, axis=-1, keepdims=True)
        prob_ref[slot] = p.astype(jnp.bfloat16)
    for slot, hd, delta, start, win_row, tok_q in items:
        attn_ref[hd, tok_q:tok_q + GRID_W, :] = jnp.dot(
            prob_ref[slot], v_ref[hd, pl.ds(start, WIN_TOKENS), :],
            preferred_element_type=jnp.float32)

    attn = jnp.concatenate([attn_ref[hd] for hd in range(N_HEADS)], axis=1)
    mixed = mix_ref[...] + ga_ref[...] * attn
    y = jnp.dot(mixed.astype(jnp.bfloat16), w_out_ref[...],
                preferred_element_type=jnp.float32)
    o_ref[...] = x_ref[...] + _rmsnorm(y, g_post_ref[...])


def _bias_table(rpb):
    n_dr = 2 * NA_WIN_H - 1
    period = 2 * GRID_W - 1
    rpb = rpb.astype(jnp.float32)
    ring = jnp.concatenate(
        [rpb[..., NA_WIN_W - 1:],
         jnp.full((N_HEADS, n_dr, period - (2 * NA_WIN_W - 1)), MASK_VALUE, jnp.float32),
         rpb[..., :NA_WIN_W - 1]], axis=-1)
    tiled = jnp.tile(ring, (1, 1, GRID_W))[..., :GRID_W * (period - 1)]
    toeplitz = tiled.reshape(N_HEADS, n_dr, GRID_W, period - 1)[..., :GRID_W]
    c = jnp.arange(GRID_W)
    col_start = jnp.clip(c - NA_WIN_W // 2, 0, GRID_W - NA_WIN_W)
    kc = jnp.arange(GRID_W)
    valid = (kc[None, :] >= col_start[:, None]) & (kc[None, :] < col_start[:, None] + NA_WIN_W)
    toeplitz = jnp.where(valid[None, None], toeplitz, MASK_VALUE)
    t = jnp.stack([toeplitz[:, NA_WIN_H - 1 - d:2 * NA_WIN_H - 1 - d] for d in range(NA_WIN_H)])
    t = jnp.transpose(t, (0, 1, 3, 2, 4))
    return t.reshape(NA_WIN_H, N_HEADS, GRID_W, WIN_TOKENS)


def _mixer(x, k, v, g_pre, w_uq, w_gate, bias, w_pool, pool_scale, w_out, g_post):
    batch, seq, _ = x.shape

    def prev_halo(halo, width):
        per_tile = TOKEN_TILE // halo
        return pl.BlockSpec((None, halo, width),
                            lambda b, i: (b, jnp.maximum(i * per_tile - 1, 0), 0))

    def next_halo(halo, width):
        per_tile = TOKEN_TILE // halo
        last = seq // halo - 1
        return pl.BlockSpec((None, halo, width),
                            lambda b, i: (b, jnp.minimum((i + 1) * per_tile, last), 0))

    def kv_pieces(token_axis):
        per_tile = TOKEN_TILE // KV_HALO
        last = seq // KV_HALO - 1

        def spec(tokens, block_index):
            if token_axis == 3:
                return pl.BlockSpec((None, K_GROUPS, 128, tokens),
                                    lambda b, i: (b, 0, 0, block_index(i)))
            return pl.BlockSpec((None, N_HEADS, tokens, 128),
                                lambda b, i: (b, 0, block_index(i), 0))

        return [spec(KV_HALO, lambda i: jnp.maximum(i * per_tile - 1, 0)),
                spec(TOKEN_TILE, lambda i: i),
                spec(KV_HALO, lambda i: jnp.minimum((i + 1) * per_tile, last))]

    in_specs = [
        _tile(D_MODEL), prev_halo(POOL_HALO, D_MODEL), next_halo(POOL_HALO, D_MODEL),
        *kv_pieces(3), *kv_pieces(2),
        _resident((1, D_MODEL)),
        _resident((D_MODEL, D_POOL + D_ATT_QK)),
        _resident((D_MODEL, 2 * D_MODEL)),
        _resident((NA_WIN_H, N_HEADS, GRID_W, WIN_TOKENS)),
        _resident((POOL_GROUPS, POOL_GROUP_C, POOL_OUT_C)),
        _resident((1, D_MODEL)),
        _resident((D_MODEL, D_MODEL)),
        _resident((1, D_MODEL)),
    ]
    n_slots = ROWS_PER_TILE * N_HEADS
    return pl.pallas_call(
        functools.partial(_mixer_kernel, seq=seq),
        grid=(batch, seq // TOKEN_TILE),
        in_specs=in_specs,
        out_specs=_tile(D_MODEL),
        out_shape=jax.ShapeDtypeStruct((batch, seq, D_MODEL), jnp.float32),
        scratch_shapes=[
            pltpu.VMEM((TOKEN_TILE + 2 * POOL_HALO, D_MODEL), jnp.bfloat16),
            pltpu.VMEM((POOL_GROUPS, TOKEN_TILE + 2 * POOL_HALO, POOL_GROUP_C),
                       jnp.float32),
            pltpu.VMEM((N_HEADS // 2, TOKEN_TILE, 128), jnp.bfloat16),
            pltpu.VMEM((TOKEN_TILE, D_MODEL), jnp.float32),
            pltpu.VMEM((TOKEN_TILE, D_MODEL), jnp.float32),
            pltpu.VMEM((N_HEADS, TOKEN_TILE, HEAD_DIM_V), jnp.float32),
            pltpu.VMEM((n_slots, GRID_W, WIN_TOKENS), jnp.float32),
            pltpu.VMEM((n_slots, GRID_W, WIN_TOKENS), jnp.bfloat16),
            pltpu.VMEM((2, K_GROUPS, LANE_BLOCKS, 128, 128), jnp.bfloat16),
            pltpu.VMEM((N_HEADS, KV_WINDOW, HEAD_DIM_V), jnp.bfloat16),
        ],
        compiler_params=_params(),
        name="mixer",
    )(x, x, x, k, k, k, v, v, v, g_pre, w_uq, w_gate, bias, w_pool, pool_scale, w_out, g_post)


def _ffn_kernel(x_ref, g_pre_ref, w_gu_ref, w_down_ref, g_post_ref, o_ref, act_ref):
    for r0 in range(0, FFN_TILE, FFN_SUBTILE):
        rows = slice(r0, r0 + FFN_SUBTILE)
        x = x_ref[rows, :]
        h = _rmsnorm(x, g_pre_ref[...]).astype(jnp.bfloat16)
        for c in range(0, D_FF, FF_CHUNK):
            gate = jnp.dot(h, w_gu_ref[:, c:c + FF_CHUNK], preferred_element_type=jnp.float32)
            up = jnp.dot(h, w_gu_ref[:, D_FF + c:D_FF + c + FF_CHUNK],
                         preferred_element_type=jnp.float32)
            act_ref[rows, c:c + FF_CHUNK] = (jax.nn.silu(gate) * up).astype(jnp.bfloat16)
        f = jnp.dot(act_ref[rows, :], w_down_ref[...], preferred_element_type=jnp.float32)
        o_ref[rows, :] = x + _rmsnorm(f, g_post_ref[...])


def _ffn(x, g_pre, w_gu, w_down, g_post):
    batch, seq, _ = x.shape
    return pl.pallas_call(
        _ffn_kernel,
        grid=(batch, seq // FFN_TILE),
        in_specs=[_tile(D_MODEL, FFN_TILE), _resident((1, D_MODEL)),
                  _resident((D_MODEL, 2 * D_FF)), _resident((D_FF, D_MODEL)),
                  _resident((1, D_MODEL))],
        out_specs=_tile(D_MODEL, FFN_TILE),
        out_shape=jax.ShapeDtypeStruct((batch, seq, D_MODEL), jnp.float32),
        scratch_shapes=[pltpu.VMEM((FFN_TILE, D_FF), jnp.bfloat16)],
        compiler_params=_params(),
        name="ffn",
    )(x, g_pre, w_gu, w_down, g_post)


def _layer(x, p):
    seq = x.shape[1]
    assert seq % TOKEN_TILE == 0 and seq % KV_PROJ_TILE == 0 and seq % FFN_TILE == 0
    assert seq >= KV_WINDOW and seq // GRID_W >= NA_WIN_H
    k, v = _kv_proj(x, p["g_mix_pre"], p["wk_t"], p["w_v"])
    x = _mixer(x, k, v, p["g_mix_pre"], p["w_uq"], p["w_gate"], p["bias"], p["w_pool"],
               p["pool_scale"], p["w_out"], p["g_mix_post"])
    return _ffn(x, p["g_ffn_pre"], p["w_gu"], p["w_down"], p["g_ffn_post"])


def kernel(x_prompt, x_sample, norm_mix_pre, w_in, w_pool_grp, pool_scale, attn_rpb, w_out,
           norm_mix_post, norm_ffn_pre, w_gate_up, w_down, norm_ffn_post):
    depth = w_in.shape[0]
    bf16 = jnp.bfloat16
    layers = []
    for l in range(depth):
        w_in_l = w_in[l].astype(bf16)
        layers.append(dict(
            g_mix_pre=norm_mix_pre[l][None, :],
            w_uq=w_in_l[:, :O_K],
            wk_t=w_in_l[:, O_K:O_V].T,
            w_v=w_in_l[:, O_V:O_GP],
            w_gate=w_in_l[:, O_GP:],
            w_pool=w_pool_grp[l].astype(bf16),
            pool_scale=pool_scale[l][None, :],
            bias=_bias_table(attn_rpb[l]),
            w_out=w_out[l].astype(bf16),
            g_mix_post=norm_mix_post[l][None, :],
            g_ffn_pre=norm_ffn_pre[l][None, :],
            w_gu=w_gate_up[l].astype(bf16),
            w_down=w_down[l].astype(bf16),
            g_ffn_post=norm_ffn_post[l][None, :],
        ))

    def trunk(x):
        for p in layers:
            x = _layer(x, p)
        return x

    return (trunk(x_prompt), trunk(x_sample))
```

```python
import functools

import jax
import jax.numpy as jnp
from jax import lax
from jax.experimental import pallas as pl
from jax.experimental.pallas import tpu as pltpu

D_MODEL = 1024
GRID_W = 64
POOL_WINDOWS = (2, 4, 8, 16)
POOL_GROUPS = 4
D_POOL = D_MODEL // 2
POOL_GROUP_C = D_POOL // POOL_GROUPS
POOL_OUT_C = D_MODEL // POOL_GROUPS
N_HEADS = D_MODEL // 128
D_ATT_QK = D_MODEL // 2
HEAD_DIM_QK = D_ATT_QK // N_HEADS
HEAD_DIM_V = D_MODEL // N_HEADS
NA_WIN_H = 8
NA_WIN_W = 16
D_IN = D_POOL + 2 * D_ATT_QK + D_MODEL + 2 * D_MODEL
D_FF = ((8 * D_MODEL + 3 * 256 - 1) // (3 * 256)) * 256
EPS = 1e-6

O_Q = D_POOL
O_K = O_Q + D_ATT_QK
O_V = O_K + D_ATT_QK
O_GP = O_V + D_MODEL
O_GA = O_GP + D_MODEL

QK_SCALE = HEAD_DIM_QK ** -0.5
MASK_VALUE = -1e30

TOKEN_TILE = 512
KV_PROJ_TILE = 1024
FFN_TILE = 1024
FFN_SUBTILE = 512
ROWS_PER_TILE = TOKEN_TILE // GRID_W
KV_HALO = (NA_WIN_H // 2) * GRID_W
KV_WINDOW = KV_HALO + TOKEN_TILE + KV_HALO
WIN_TOKENS = NA_WIN_H * GRID_W
K_GROUPS = N_HEADS // 2
LANE_BLOCKS = KV_WINDOW // 128
POOL_HALO = 16
FF_CHUNK = 256
VMEM_LIMIT_BYTES = 58 * 1024 * 1024


def _rmsnorm(x, g):
    y = x * lax.rsqrt(jnp.mean(x * x, axis=-1, keepdims=True) + EPS)
    return y * g


def _resident(shape):
    zeros = (0,) * len(shape)
    return pl.BlockSpec(shape, lambda b, i: zeros, pipeline_mode=pl.Buffered(1))


def _tile(width, tokens=TOKEN_TILE):
    return pl.BlockSpec((None, tokens, width), lambda b, i: (b, i, 0))


def _params():
    return pltpu.CompilerParams(dimension_semantics=("arbitrary", "arbitrary"),
                                vmem_limit_bytes=VMEM_LIMIT_BYTES)


def _kv_proj_kernel(x_ref, g_ref, wk_t_ref, wv_ref, kt_ref, v_ref):
    h = _rmsnorm(x_ref[...], g_ref[...]).astype(jnp.bfloat16)
    k_t = lax.dot_general(wk_t_ref[...], h, (((1,), (1,)), ((), ())),
                          preferred_element_type=jnp.float32).astype(jnp.bfloat16)
    for grp in range(kt_ref.shape[0]):
        kt_ref[grp] = k_t[grp * 128:(grp + 1) * 128, :]
    for c in range(0, D_MODEL, 512):
        y = jnp.dot(h, wv_ref[:, c:c + 512],
                    preferred_element_type=jnp.float32).astype(jnp.bfloat16)
        for j in range(512 // 128):
            v_ref[c // 128 + j] = y[:, j * 128:(j + 1) * 128]


def _kv_proj(x, g_pre, wk_t, w_v):
    batch, seq, _ = x.shape
    return pl.pallas_call(
        _kv_proj_kernel,
        grid=(batch, seq // KV_PROJ_TILE),
        in_specs=[_tile(D_MODEL, KV_PROJ_TILE), _resident((1, D_MODEL)),
                  _resident((D_ATT_QK, D_MODEL)), _resident((D_MODEL, D_MODEL))],
        out_specs=(
            pl.BlockSpec((None, K_GROUPS, 128, KV_PROJ_TILE), lambda b, i: (b, 0, 0, i)),
            pl.BlockSpec((None, N_HEADS, KV_PROJ_TILE, 128), lambda b, i: (b, 0, i, 0))),
        out_shape=(
            jax.ShapeDtypeStruct((batch, K_GROUPS, 128, seq), jnp.bfloat16),
            jax.ShapeDtypeStruct((batch, N_HEADS, seq, 128), jnp.bfloat16)),
        compiler_params=_params(),
        name="kv_proj",
    )(x, g_pre, wk_t, w_v)


def _mixer_kernel(x_ref, x_prev_ref, x_next_ref, k_prev_ref, k_cur_ref, k_next_ref,
                  v_prev_ref, v_cur_ref, v_next_ref, g_pre_ref, w_uq_ref, w_gate_ref,
                  bias_ref, w_pool_ref, pool_scale_ref, w_out_ref, g_post_ref,
                  o_ref, h_ref, upad_ref, q_ref, ga_ref, mix_ref, attn_ref, score_ref, prob_ref,
                  kt_ref, v_ref, denom_ref, *, seq):
    i = pl.program_id(1)
    rows = seq // GRID_W
    tile_rows = slice(POOL_HALO, POOL_HALO + TOKEN_TILE)

    g_pre = g_pre_ref[...]
    h_ref[0:POOL_HALO, :] = _rmsnorm(x_prev_ref[...], g_pre).astype(jnp.bfloat16)
    h_ref[tile_rows, :] = _rmsnorm(x_ref[...], g_pre).astype(jnp.bfloat16)
    h_ref[POOL_HALO + TOKEN_TILE:, :] = _rmsnorm(x_next_ref[...], g_pre).astype(jnp.bfloat16)

    h = h_ref[tile_rows, :]
    q = (jnp.dot(h, w_uq_ref[:, D_POOL:], preferred_element_type=jnp.float32)
         * QK_SCALE).astype(jnp.bfloat16)
    for grp in range(N_HEADS // 2):
        q_ref[grp] = q[:, grp * 128:(grp + 1) * 128]

    lane = lax.broadcasted_iota(jnp.int32, (GRID_W, 128), 1)
    pieces = ((0, KV_HALO), (KV_HALO, TOKEN_TILE), (KV_HALO + TOKEN_TILE, KV_HALO))
    for src, (row0, n_rows) in zip((v_prev_ref, v_cur_ref, v_next_ref), pieces):
        v_ref[:, row0:row0 + n_rows, :] = src[...]
    for grp in range(K_GROUPS):
        k_t = jnp.concatenate([k_prev_ref[grp], k_cur_ref[grp], k_next_ref[grp]], axis=1)
        shifted = pltpu.bitcast(
            pltpu.roll(pltpu.bitcast(k_t, jnp.uint32), KV_WINDOW - GRID_W, axis=1), jnp.bfloat16)
        for blk in range(LANE_BLOCKS):
            lanes = slice(blk * 128, (blk + 1) * 128)
            kt_ref[0, grp, blk] = k_t[:, lanes]
            kt_ref[1, grp, blk] = shifted[:, lanes]
    first_row_in_window = i * ROWS_PER_TILE - KV_HALO // GRID_W

    items = []
    for rr in range(ROWS_PER_TILE):
        r = i * ROWS_PER_TILE + rr
        row_start = jnp.clip(r - NA_WIN_H // 2, 0, rows - NA_WIN_H)
        delta = r - row_start
        win_row = row_start - first_row_in_window
        start = pl.multiple_of(win_row * GRID_W, GRID_W)
        for hd in range(N_HEADS):
            items.append((rr * N_HEADS + hd, hd, delta, start, win_row, rr * GRID_W))

    first_head_lanes = lane < HEAD_DIM_QK
    no_q = jnp.zeros((), jnp.bfloat16)
    for slot, hd, delta, start, win_row, tok_q in items[::2]:
        q_pair = q_ref[hd // 2, tok_q:tok_q + GRID_W, :]
        q_both = jnp.concatenate([jnp.where(first_head_lanes, q_pair, no_q),
                                  jnp.where(first_head_lanes, no_q, q_pair)], axis=0)
        k_t = jnp.concatenate(
            [kt_ref[win_row % 2, hd // 2, win_row // 2 + blk] for blk in range(WIN_TOKENS // 128)],
            axis=1)
        s = jnp.dot(q_both, k_t, preferred_element_type=jnp.float32)
        score_ref[slot] = s[:GRID_W] + bias_ref[delta, hd]
        score_ref[slot + 1] = s[GRID_W:] + bias_ref[delta, hd + 1]

    u_ext = jnp.dot(h_ref[...], w_uq_ref[:, :D_POOL], preferred_element_type=jnp.float32)
    tok = (i * TOKEN_TILE - POOL_HALO
           + lax.broadcasted_iota(jnp.int32, (TOKEN_TILE + 2 * POOL_HALO, POOL_GROUP_C), 0))
    in_seq = (tok >= 0) & (tok < seq)
    for g in range(POOL_GROUPS):
        upad_ref[g] = jnp.where(in_seq, u_ext[:, g * POOL_GROUP_C:(g + 1) * POOL_GROUP_C], 0.0)
    for c in range(0, D_MODEL, 512):
        ga_ref[:, c:c + 512] = jax.nn.sigmoid(
            jnp.dot(h, w_gate_ref[:, D_MODEL + c:D_MODEL + c + 512],
                    preferred_element_type=jnp.float32))

    t = i * TOKEN_TILE + lax.broadcasted_iota(jnp.int32, (TOKEN_TILE, POOL_GROUP_C), 0)
    for g, w in enumerate(POOL_WINDOWS):
        half = w // 2
        win_sum = upad_ref[g, POOL_HALO - half:POOL_HALO - half + TOKEN_TILE, :]
        for d in range(-half + 1, half):
            win_sum = win_sum + upad_ref[g, POOL_HALO + d:POOL_HALO + d + TOKEN_TILE, :]
        cnt = jnp.minimum(t + half, seq) - jnp.maximum(t - half, 0)
        pooled = win_sum / cnt.astype(jnp.float32) - upad_ref[g, tile_rows, :]
        y = jnp.dot(pooled.astype(jnp.bfloat16), w_pool_ref[g],
                    preferred_element_type=jnp.float32)
        cols = slice(g * POOL_OUT_C, (g + 1) * POOL_OUT_C)
        gate = jax.nn.sigmoid(jnp.dot(h, w_gate_ref[:, cols], preferred_element_type=jnp.float32))
        mix_ref[:, cols] = gate * (y * pool_scale_ref[:, cols])

    for slot, hd, delta, start, win_row, tok_q in items:
        s = score_ref[slot]
        e = jnp.exp(s - jnp.max(s, axis=-1, keepdims=True))
        denom_ref[slot] = jnp.broadcast_to(jnp.sum(e, axis=-1, keepdims=True),
                                           (GRID_W, HEAD_DIM_V))
        prob_ref[slot] = e.astype(jnp.bfloat16)
    for slot, hd, delta, start, win_row, tok_q in items:
        o = jnp.dot(prob_ref[slot], v_ref[hd, pl.ds(start, WIN_TOKENS), :],
                    preferred_element_type=jnp.float32)
        attn_ref[hd, tok_q:tok_q + GRID_W, :] = o / denom_ref[slot]

    attn = jnp.concatenate([attn_ref[hd] for hd in range(N_HEADS)], axis=1)
    mixed = mix_ref[...] + ga_ref[...] * attn
    y = jnp.dot(mixed.astype(jnp.bfloat16), w_out_ref[...],
                preferred_element_type=jnp.float32)
    o_ref[...] = x_ref[...] + _rmsnorm(y, g_post_ref[...])


def _bias_table(rpb):
    n_dr = 2 * NA_WIN_H - 1
    period = 2 * GRID_W - 1
    rpb = rpb.astype(jnp.float32)
    ring = jnp.concatenate(
        [rpb[..., NA_WIN_W - 1:],
         jnp.full((N_HEADS, n_dr, period - (2 * NA_WIN_W - 1)), MASK_VALUE, jnp.float32),
         rpb[..., :NA_WIN_W - 1]], axis=-1)
    tiled = jnp.tile(ring, (1, 1, GRID_W))[..., :GRID_W * (period - 1)]
    toeplitz = tiled.reshape(N_HEADS, n_dr, GRID_W, period - 1)[..., :GRID_W]
    c = jnp.arange(GRID_W)
    col_start = jnp.clip(c - NA_WIN_W // 2, 0, GRID_W - NA_WIN_W)
    kc = jnp.arange(GRID_W)
    valid = (kc[None, :] >= col_start[:, None]) & (kc[None, :] < col_start[:, None] + NA_WIN_W)
    toeplitz = jnp.where(valid[None, None], toeplitz, MASK_VALUE)
    t = jnp.stack([toeplitz[:, NA_WIN_H - 1 - d:2 * NA_WIN_H - 1 - d] for d in range(NA_WIN_H)])
    t = jnp.transpose(t, (0, 1, 3, 2, 4))
    return t.reshape(NA_WIN_H, N_HEADS, GRID_W, WIN_TOKENS)


def _mixer(x, k, v, g_pre, w_uq, w_gate, bias, w_pool, pool_scale, w_out, g_post):
    batch, seq, _ = x.shape

    def prev_halo(halo, width):
        per_tile = TOKEN_TILE // halo
        return pl.BlockSpec((None, halo, width),
                            lambda b, i: (b, jnp.maximum(i * per_tile - 1, 0), 0))

    def next_halo(halo, width):
        per_tile = TOKEN_TILE // halo
        last = seq // halo - 1
        return pl.BlockSpec((None, halo, width),
                            lambda b, i: (b, jnp.minimum((i + 1) * per_tile, last), 0))

    def kv_pieces(token_axis):
        per_tile = TOKEN_TILE // KV_HALO
        last = seq // KV_HALO - 1

        def spec(tokens, block_index):
            if token_axis == 3:
                return pl.BlockSpec((None, K_GROUPS, 128, tokens),
                                    lambda b, i: (b, 0, 0, block_index(i)))
            return pl.BlockSpec((None, N_HEADS, tokens, 128),
                                lambda b, i: (b, 0, block_index(i), 0))

        return [spec(KV_HALO, lambda i: jnp.maximum(i * per_tile - 1, 0)),
                spec(TOKEN_TILE, lambda i: i),
                spec(KV_HALO, lambda i: jnp.minimum((i + 1) * per_tile, last))]

    in_specs = [
        _tile(D_MODEL), prev_halo(POOL_HALO, D_MODEL), next_halo(POOL_HALO, D_MODEL),
        *kv_pieces(3), *kv_pieces(2),
        _resident((1, D_MODEL)),
        _resident((D_MODEL, D_POOL + D_ATT_QK)),
        _resident((D_MODEL, 2 * D_MODEL)),
        _resident((NA_WIN_H, N_HEADS, GRID_W, WIN_TOKENS)),
        _resident((POOL_GROUPS, POOL_GROUP_C, POOL_OUT_C)),
        _resident((1, D_MODEL)),
        _resident((D_MODEL, D_MODEL)),
        _resident((1, D_MODEL)),
    ]
    n_slots = ROWS_PER_TILE * N_HEADS
    return pl.pallas_call(
        functools.partial(_mixer_kernel, seq=seq),
        grid=(batch, seq // TOKEN_TILE),
        in_specs=in_specs,
        out_specs=_tile(D_MODEL),
        out_shape=jax.ShapeDtypeStruct((batch, seq, D_MODEL), jnp.float32),
        scratch_shapes=[
            pltpu.VMEM((TOKEN_TILE + 2 * POOL_HALO, D_MODEL), jnp.bfloat16),
            pltpu.VMEM((POOL_GROUPS, TOKEN_TILE + 2 * POOL_HALO, POOL_GROUP_C),
                       jnp.float32),
            pltpu.VMEM((N_HEADS // 2, TOKEN_TILE, 128), jnp.bfloat16),
            pltpu.VMEM((TOKEN_TILE, D_MODEL), jnp.float32),
            pltpu.VMEM((TOKEN_TILE, D_MODEL), jnp.float32),
            pltpu.VMEM((N_HEADS, TOKEN_TILE, HEAD_DIM_V), jnp.float32),
            pltpu.VMEM((n_slots, GRID_W, WIN_TOKENS), jnp.float32),
            pltpu.VMEM((n_slots, GRID_W, WIN_TOKENS), jnp.bfloat16),
            pltpu.VMEM((2, K_GROUPS, LANE_BLOCKS, 128, 128), jnp.bfloat16),
            pltpu.VMEM((N_HEADS, KV_WINDOW, HEAD_DIM_V), jnp.bfloat16),
            pltpu.VMEM((n_slots, GRID_W, HEAD_DIM_V), jnp.float32),
        ],
        compiler_params=_params(),
        name="mixer",
    )(x, x, x, k, k, k, v, v, v, g_pre, w_uq, w_gate, bias, w_pool, pool_scale, w_out, g_post)


def _ffn_kernel(x_ref, g_pre_ref, w_gu_ref, w_down_ref, g_post_ref, o_ref, act_ref):
    for r0 in range(0, FFN_TILE, FFN_SUBTILE):
        rows = slice(r0, r0 + FFN_SUBTILE)
        x = x_ref[rows, :]
        h = _rmsnorm(x, g_pre_ref[...]).astype(jnp.bfloat16)
        for c in range(0, D_FF, FF_CHUNK):
            gate = jnp.dot(h, w_gu_ref[:, c:c + FF_CHUNK], preferred_element_type=jnp.float32)
            up = jnp.dot(h, w_gu_ref[:, D_FF + c:D_FF + c + FF_CHUNK],
                         preferred_element_type=jnp.float32)
            act_ref[rows, c:c + FF_CHUNK] = (jax.nn.silu(gate) * up).astype(jnp.bfloat16)
        f = jnp.dot(act_ref[rows, :], w_down_ref[...], preferred_element_type=jnp.float32)
        o_ref[rows, :] = x + _rmsnorm(f, g_post_ref[...])


def _ffn(x, g_pre, w_gu, w_down, g_post):
    batch, seq, _ = x.shape
    return pl.pallas_call(
        _ffn_kernel,
        grid=(batch, seq // FFN_TILE),
        in_specs=[_tile(D_MODEL, FFN_TILE), _resident((1, D_MODEL)),
                  _resident((D_MODEL, 2 * D_FF)), _resident((D_FF, D_MODEL)),
                  _resident((1, D_MODEL))],
        out_specs=_tile(D_MODEL, FFN_TILE),
        out_shape=jax.ShapeDtypeStruct((batch, seq, D_MODEL), jnp.float32),
        scratch_shapes=[pltpu.VMEM((FFN_TILE, D_FF), jnp.bfloat16)],
        compiler_params=_params(),
        name="ffn",
    )(x, g_pre, w_gu, w_down, g_post)


def _layer(x, p):
    seq = x.shape[1]
    assert seq % TOKEN_TILE == 0 and seq % KV_PROJ_TILE == 0 and seq % FFN_TILE == 0
    assert seq >= KV_WINDOW and seq // GRID_W >= NA_WIN_H
    k, v = _kv_proj(x, p["g_mix_pre"], p["wk_t"], p["w_v"])
    x = _mixer(x, k, v, p["g_mix_pre"], p["w_uq"], p["w_gate"], p["bias"], p["w_pool"],
               p["pool_scale"], p["w_out"], p["g_mix_post"])
    return _ffn(x, p["g_ffn_pre"], p["w_gu"], p["w_down"], p["g_ffn_post"])


def kernel(x_prompt, x_sample, norm_mix_pre, w_in, w_pool_grp, pool_scale, attn_rpb, w_out,
           norm_mix_post, norm_ffn_pre, w_gate_up, w_down, norm_ffn_post):
    depth = w_in.shape[0]
    bf16 = jnp.bfloat16
    layers = []
    for l in range(depth):
        w_in_l = w_in[l].astype(bf16)
        layers.append(dict(
            g_mix_pre=norm_mix_pre[l][None, :],
            w_uq=w_in_l[:, :O_K],
            wk_t=w_in_l[:, O_K:O_V].T,
            w_v=w_in_l[:, O_V:O_GP],
            w_gate=w_in_l[:, O_GP:],
            w_pool=w_pool_grp[l].astype(bf16),
            pool_scale=pool_scale[l][None, :],
            bias=_bias_table(attn_rpb[l]),
            w_out=w_out[l].astype(bf16),
            g_mix_post=norm_mix_post[l][None, :],
            g_ffn_pre=norm_ffn_pre[l][None, :],
            w_gu=w_gate_up[l].astype(bf16),
            w_down=w_down[l].astype(bf16),
            g_ffn_post=norm_ffn_post[l][None, :],
        ))

    def trunk(x):
        for p in layers:
            x = _layer(x, p)
        return x

    return (trunk(x_prompt), trunk(x_sample))
```

```python
import functools

import jax
import jax.numpy as jnp
from jax import lax
from jax.experimental import pallas as pl
from jax.experimental.pallas import tpu as pltpu

D_MODEL = 1024
GRID_W = 64
POOL_WINDOWS = (2, 4, 8, 16)
POOL_GROUPS = 4
D_POOL = D_MODEL // 2
POOL_GROUP_C = D_POOL // POOL_GROUPS
POOL_OUT_C = D_MODEL // POOL_GROUPS
N_HEADS = D_MODEL // 128
D_ATT_QK = D_MODEL // 2
HEAD_DIM_QK = D_ATT_QK // N_HEADS
HEAD_DIM_V = D_MODEL // N_HEADS
NA_WIN_H = 8
NA_WIN_W = 16
D_IN = D_POOL + 2 * D_ATT_QK + D_MODEL + 2 * D_MODEL
D_FF = ((8 * D_MODEL + 3 * 256 - 1) // (3 * 256)) * 256
EPS = 1e-6

O_Q = D_POOL
O_K = O_Q + D_ATT_QK
O_V = O_K + D_ATT_QK
O_GP = O_V + D_MODEL
O_GA = O_GP + D_MODEL

QK_SCALE = HEAD_DIM_QK ** -0.5
MASK_VALUE = -1e30

LANES = 128
PROJ_CHUNK = 512

TOKEN_TILE = 512
KV_PROJ_TILE = 1024
FFN_TILE = 1024
FFN_SUBTILE = 512
ROWS_PER_TILE = TOKEN_TILE // GRID_W
KV_HALO = (NA_WIN_H // 2) * GRID_W
KV_WINDOW = KV_HALO + TOKEN_TILE + KV_HALO
WIN_TOKENS = NA_WIN_H * GRID_W
K_GROUPS = D_ATT_QK // LANES
LANE_BLOCKS = KV_WINDOW // LANES
POOL_HALO = 16
FF_CHUNK = 256
VMEM_LIMIT_BYTES = 58 * 1024 * 1024


def _rmsnorm(x, g):
    y = x * lax.rsqrt(jnp.mean(x * x, axis=-1, keepdims=True) + EPS)
    return y * g


def _resident(shape):
    zeros = (0,) * len(shape)
    return pl.BlockSpec(shape, lambda b, i: zeros, pipeline_mode=pl.Buffered(1))


def _tile(width, tokens=TOKEN_TILE):
    return pl.BlockSpec((None, tokens, width), lambda b, i: (b, i, 0))


def _params():
    return pltpu.CompilerParams(dimension_semantics=("arbitrary", "arbitrary"),
                                vmem_limit_bytes=VMEM_LIMIT_BYTES)


def _kv_proj_kernel(x_ref, g_ref, wk_t_ref, wv_ref, kt_ref, v_ref):
    h = _rmsnorm(x_ref[...], g_ref[...]).astype(jnp.bfloat16)
    k_t = lax.dot_general(wk_t_ref[...], h, (((1,), (1,)), ((), ())),
                          preferred_element_type=jnp.float32).astype(jnp.bfloat16)
    for grp in range(kt_ref.shape[0]):
        kt_ref[grp] = k_t[grp * LANES:(grp + 1) * LANES, :]
    for c in range(0, D_MODEL, PROJ_CHUNK):
        y = jnp.dot(h, wv_ref[:, c:c + PROJ_CHUNK],
                    preferred_element_type=jnp.float32).astype(jnp.bfloat16)
        for j in range(PROJ_CHUNK // HEAD_DIM_V):
            v_ref[c // HEAD_DIM_V + j] = y[:, j * HEAD_DIM_V:(j + 1) * HEAD_DIM_V]


def _kv_proj(x, g_pre, wk_t, w_v):
    batch, seq, _ = x.shape
    return pl.pallas_call(
        _kv_proj_kernel,
        grid=(batch, seq // KV_PROJ_TILE),
        in_specs=[_tile(D_MODEL, KV_PROJ_TILE), _resident((1, D_MODEL)),
                  _resident((D_ATT_QK, D_MODEL)), _resident((D_MODEL, D_MODEL))],
        out_specs=(
            pl.BlockSpec((None, K_GROUPS, LANES, KV_PROJ_TILE), lambda b, i: (b, 0, 0, i)),
            pl.BlockSpec((None, N_HEADS, KV_PROJ_TILE, HEAD_DIM_V), lambda b, i: (b, 0, i, 0))),
        out_shape=(
            jax.ShapeDtypeStruct((batch, K_GROUPS, LANES, seq), jnp.bfloat16),
            jax.ShapeDtypeStruct((batch, N_HEADS, seq, HEAD_DIM_V), jnp.bfloat16)),
        compiler_params=_params(),
        name="kv_proj",
    )(x, g_pre, wk_t, w_v)


def _mixer_kernel(x_ref, x_prev_ref, x_next_ref, k_prev_ref, k_cur_ref, k_next_ref,
                  v_prev_ref, v_cur_ref, v_next_ref, g_pre_ref, w_uq_ref, w_gate_ref,
                  bias_ref, w_pool_ref, pool_scale_ref, w_out_ref, g_post_ref,
                  o_ref, h_ref, upad_ref, q_ref, ga_ref, mix_ref, attn_ref, score_ref, prob_ref,
                  kt_ref, v_ref, denom_ref, *, seq):
    i = pl.program_id(1)
    rows = seq // GRID_W
    tile_rows = slice(POOL_HALO, POOL_HALO + TOKEN_TILE)

    g_pre = g_pre_ref[...]
    h_ref[0:POOL_HALO, :] = _rmsnorm(x_prev_ref[...], g_pre).astype(jnp.bfloat16)
    h_ref[tile_rows, :] = _rmsnorm(x_ref[...], g_pre).astype(jnp.bfloat16)
    h_ref[POOL_HALO + TOKEN_TILE:, :] = _rmsnorm(x_next_ref[...], g_pre).astype(jnp.bfloat16)

    h = h_ref[tile_rows, :]
    q = (jnp.dot(h, w_uq_ref[:, D_POOL:], preferred_element_type=jnp.float32)
         * QK_SCALE).astype(jnp.bfloat16)
    for grp in range(K_GROUPS):
        q_ref[grp] = q[:, grp * LANES:(grp + 1) * LANES]

    lane = lax.broadcasted_iota(jnp.int32, (GRID_W, LANES), 1)
    pieces = ((0, KV_HALO), (KV_HALO, TOKEN_TILE), (KV_HALO + TOKEN_TILE, KV_HALO))
    for src, (row0, n_rows) in zip((v_prev_ref, v_cur_ref, v_next_ref), pieces):
        v_ref[:, row0:row0 + n_rows, :] = src[...]
    for grp in range(K_GROUPS):
        k_t = jnp.concatenate([k_prev_ref[grp], k_cur_ref[grp], k_next_ref[grp]], axis=1)
        shifted = pltpu.bitcast(
            pltpu.roll(pltpu.bitcast(k_t, jnp.uint32), KV_WINDOW - GRID_W, axis=1), jnp.bfloat16)
        for blk in range(LANE_BLOCKS):
            lanes = slice(blk * LANES, (blk + 1) * LANES)
            kt_ref[0, grp, blk] = k_t[:, lanes]
            kt_ref[1, grp, blk] = shifted[:, lanes]
    first_row_in_window = i * ROWS_PER_TILE - KV_HALO // GRID_W

    items = []
    for rr in range(ROWS_PER_TILE):
        r = i * ROWS_PER_TILE + rr
        row_start = jnp.clip(r - NA_WIN_H // 2, 0, rows - NA_WIN_H)
        delta = r - row_start
        win_row = row_start - first_row_in_window
        start = pl.multiple_of(win_row * GRID_W, GRID_W)
        for hd in range(N_HEADS):
            items.append((rr * N_HEADS + hd, hd, delta, start, win_row, rr * GRID_W))

    first_head_lanes = lane < HEAD_DIM_QK
    no_q = jnp.zeros((), jnp.bfloat16)
    for slot, hd, delta, start, win_row, tok_q in items[::2]:
        q_pair = q_ref[hd // 2, tok_q:tok_q + GRID_W, :]
        q_both = jnp.concatenate([jnp.where(first_head_lanes, q_pair, no_q),
                                  jnp.where(first_head_lanes, no_q, q_pair)], axis=0)
        k_t = jnp.concatenate(
            [kt_ref[win_row % 2, hd // 2, win_row // 2 + blk] for blk in range(WIN_TOKENS // LANES)],
            axis=1)
        s = jnp.dot(q_both, k_t, preferred_element_type=jnp.float32)
        score_ref[slot] = s[:GRID_W] + bias_ref[delta, hd]
        score_ref[slot + 1] = s[GRID_W:] + bias_ref[delta, hd + 1]

    u_ext = jnp.dot(h_ref[...], w_uq_ref[:, :D_POOL], preferred_element_type=jnp.float32)
    tok = (i * TOKEN_TILE - POOL_HALO
           + lax.broadcasted_iota(jnp.int32, (TOKEN_TILE + 2 * POOL_HALO, POOL_GROUP_C), 0))
    in_seq = (tok >= 0) & (tok < seq)
    for g in range(POOL_GROUPS):
        upad_ref[g] = jnp.where(in_seq, u_ext[:, g * POOL_GROUP_C:(g + 1) * POOL_GROUP_C], 0.0)
    for c in range(0, D_MODEL, PROJ_CHUNK):
        ga_ref[:, c:c + PROJ_CHUNK] = jax.nn.sigmoid(
            jnp.dot(h, w_gate_ref[:, D_MODEL + c:D_MODEL + c + PROJ_CHUNK],
                    preferred_element_type=jnp.float32))

    t = i * TOKEN_TILE + lax.broadcasted_iota(jnp.int32, (TOKEN_TILE, POOL_GROUP_C), 0)
    for g, w in enumerate(POOL_WINDOWS):
        half = w // 2
        win_sum = upad_ref[g, POOL_HALO - half:POOL_HALO - half + TOKEN_TILE, :]
        for d in range(-half + 1, half):
            win_sum = win_sum + upad_ref[g, POOL_HALO + d:POOL_HALO + d + TOKEN_TILE, :]
        cnt = jnp.minimum(t + half, seq) - jnp.maximum(t - half, 0)
        pooled = win_sum / cnt.astype(jnp.float32) - upad_ref[g, tile_rows, :]
        y = jnp.dot(pooled.astype(jnp.bfloat16), w_pool_ref[g],
                    preferred_element_type=jnp.float32)
        cols = slice(g * POOL_OUT_C, (g + 1) * POOL_OUT_C)
        gate = jax.nn.sigmoid(jnp.dot(h, w_gate_ref[:, cols], preferred_element_type=jnp.float32))
        mix_ref[:, cols] = gate * (y * pool_scale_ref[:, cols])

    for slot, hd, delta, start, win_row, tok_q in items:
        s = score_ref[slot]
        e = jnp.exp(s - jnp.max(s, axis=-1, keepdims=True))
        denom_ref[slot] = jnp.broadcast_to(jnp.sum(e, axis=-1, keepdims=True),
                                           (GRID_W, HEAD_DIM_V))
        prob_ref[slot] = e.astype(jnp.bfloat16)
    for slot, hd, delta, start, win_row, tok_q in items:
        o = jnp.dot(prob_ref[slot], v_ref[hd, pl.ds(start, WIN_TOKENS), :],
                    preferred_element_type=jnp.float32)
        attn_ref[hd, tok_q:tok_q + GRID_W, :] = o / denom_ref[slot]

    attn = jnp.concatenate([attn_ref[hd] for hd in range(N_HEADS)], axis=1)
    mixed = mix_ref[...] + ga_ref[...] * attn
    y = jnp.dot(mixed.astype(jnp.bfloat16), w_out_ref[...],
                preferred_element_type=jnp.float32)
    o_ref[...] = x_ref[...] + _rmsnorm(y, g_post_ref[...])


def _bias_table(rpb):
    n_dr = 2 * NA_WIN_H - 1
    period = 2 * GRID_W - 1
    rpb = rpb.astype(jnp.float32)
    ring = jnp.concatenate(
        [rpb[..., NA_WIN_W - 1:],
         jnp.full((N_HEADS, n_dr, period - (2 * NA_WIN_W - 1)), MASK_VALUE, jnp.float32),
         rpb[..., :NA_WIN_W - 1]], axis=-1)
    tiled = jnp.tile(ring, (1, 1, GRID_W))[..., :GRID_W * (period - 1)]
    toeplitz = tiled.reshape(N_HEADS, n_dr, GRID_W, period - 1)[..., :GRID_W]
    c = jnp.arange(GRID_W)
    col_start = jnp.clip(c - NA_WIN_W // 2, 0, GRID_W - NA_WIN_W)
    kc = jnp.arange(GRID_W)
    valid = (kc[None, :] >= col_start[:, None]) & (kc[None, :] < col_start[:, None] + NA_WIN_W)
    toeplitz = jnp.where(valid[None, None], toeplitz, MASK_VALUE)
    t = jnp.stack([toeplitz[:, NA_WIN_H - 1 - d:2 * NA_WIN_H - 1 - d] for d in range(NA_WIN_H)])
    t = jnp.transpose(t, (0, 1, 3, 2, 4))
    return t.reshape(NA_WIN_H, N_HEADS, GRID_W, WIN_TOKENS)


def _mixer(x, k, v, g_pre, w_uq, w_gate, bias, w_pool, pool_scale, w_out, g_post):
    batch, seq, _ = x.shape

    def prev_halo(halo, width):
        per_tile = TOKEN_TILE // halo
        return pl.BlockSpec((None, halo, width),
                            lambda b, i: (b, jnp.maximum(i * per_tile - 1, 0), 0))

    def next_halo(halo, width):
        per_tile = TOKEN_TILE // halo
        last = seq // halo - 1
        return pl.BlockSpec((None, halo, width),
                            lambda b, i: (b, jnp.minimum((i + 1) * per_tile, last), 0))

    def kv_pieces(token_axis):
        per_tile = TOKEN_TILE // KV_HALO
        last = seq // KV_HALO - 1

        def spec(tokens, block_index):
            if token_axis == 3:
                return pl.BlockSpec((None, K_GROUPS, LANES, tokens),
                                    lambda b, i: (b, 0, 0, block_index(i)))
            return pl.BlockSpec((None, N_HEADS, tokens, HEAD_DIM_V),
                                lambda b, i: (b, 0, block_index(i), 0))

        return [spec(KV_HALO, lambda i: jnp.maximum(i * per_tile - 1, 0)),
                spec(TOKEN_TILE, lambda i: i),
                spec(KV_HALO, lambda i: jnp.minimum((i + 1) * per_tile, last))]

    in_specs = [
        _tile(D_MODEL), prev_halo(POOL_HALO, D_MODEL), next_halo(POOL_HALO, D_MODEL),
        *kv_pieces(3), *kv_pieces(2),
        _resident((1, D_MODEL)),
        _resident((D_MODEL, D_POOL + D_ATT_QK)),
        _resident((D_MODEL, 2 * D_MODEL)),
        _resident((NA_WIN_H, N_HEADS, GRID_W, WIN_TOKENS)),
        _resident((POOL_GROUPS, POOL_GROUP_C, POOL_OUT_C)),
        _resident((1, D_MODEL)),
        _resident((D_MODEL, D_MODEL)),
        _resident((1, D_MODEL)),
    ]
    n_slots = ROWS_PER_TILE * N_HEADS
    return pl.pallas_call(
        functools.partial(_mixer_kernel, seq=seq),
        grid=(batch, seq // TOKEN_TILE),
        in_specs=in_specs,
        out_specs=_tile(D_MODEL),
        out_shape=jax.ShapeDtypeStruct((batch, seq, D_MODEL), jnp.float32),
        scratch_shapes=[
            pltpu.VMEM((TOKEN_TILE + 2 * POOL_HALO, D_MODEL), jnp.bfloat16),
            pltpu.VMEM((POOL_GROUPS, TOKEN_TILE + 2 * POOL_HALO, POOL_GROUP_C),
                       jnp.float32),
            pltpu.VMEM((K_GROUPS, TOKEN_TILE, LANES), jnp.bfloat16),
            pltpu.VMEM((TOKEN_TILE, D_MODEL), jnp.float32),
            pltpu.VMEM((TOKEN_TILE, D_MODEL), jnp.float32),
            pltpu.VMEM((N_HEADS, TOKEN_TILE, HEAD_DIM_V), jnp.float32),
            pltpu.VMEM((n_slots, GRID_W, WIN_TOKENS), jnp.float32),
            pltpu.VMEM((n_slots, GRID_W, WIN_TOKENS), jnp.bfloat16),
            pltpu.VMEM((2, K_GROUPS, LANE_BLOCKS, LANES, LANES), jnp.bfloat16),
            pltpu.VMEM((N_HEADS, KV_WINDOW, HEAD_DIM_V), jnp.bfloat16),
            pltpu.VMEM((n_slots, GRID_W, HEAD_DIM_V), jnp.float32),
        ],
        compiler_params=_params(),
        name="mixer",
    )(x, x, x, k, k, k, v, v, v, g_pre, w_uq, w_gate, bias, w_pool, pool_scale, w_out, g_post)


def _ffn_kernel(x_ref, g_pre_ref, w_gu_ref, w_down_ref, g_post_ref, o_ref, act_ref):
    for r0 in range(0, FFN_TILE, FFN_SUBTILE):
        rows = slice(r0, r0 + FFN_SUBTILE)
        x = x_ref[rows, :]
        h = _rmsnorm(x, g_pre_ref[...]).astype(jnp.bfloat16)
        for c in range(0, D_FF, FF_CHUNK):
            gate = jnp.dot(h, w_gu_ref[:, c:c + FF_CHUNK], preferred_element_type=jnp.float32)
            up = jnp.dot(h, w_gu_ref[:, D_FF + c:D_FF + c + FF_CHUNK],
                         preferred_element_type=jnp.float32)
            act_ref[rows, c:c + FF_CHUNK] = (jax.nn.silu(gate) * up).astype(jnp.bfloat16)
        f = jnp.dot(act_ref[rows, :], w_down_ref[...], preferred_element_type=jnp.float32)
        o_ref[rows, :] = x + _rmsnorm(f, g_post_ref[...])


def _ffn(x, g_pre, w_gu, w_down, g_post):
    batch, seq, _ = x.shape
    return pl.pallas_call(
        _ffn_kernel,
        grid=(batch, seq // FFN_TILE),
        in_specs=[_tile(D_MODEL, FFN_TILE), _resident((1, D_MODEL)),
                  _resident((D_MODEL, 2 * D_FF)), _resident((D_FF, D_MODEL)),
                  _resident((1, D_MODEL))],
        out_specs=_tile(D_MODEL, FFN_TILE),
        out_shape=jax.ShapeDtypeStruct((batch, seq, D_MODEL), jnp.float32),
        scratch_shapes=[pltpu.VMEM((FFN_TILE, D_FF), jnp.bfloat16)],
        compiler_params=_params(),
        name="ffn",
    )(x, g_pre, w_gu, w_down, g_post)


def _layer(x, p):
    seq = x.shape[1]
    assert seq % TOKEN_TILE == 0 and seq % KV_PROJ_TILE == 0 and seq % FFN_TILE == 0
    assert seq >= KV_WINDOW and seq // GRID_W >= NA_WIN_H
    k, v = _kv_proj(x, p["g_mix_pre"], p["wk_t"], p["w_v"])
    x = _mixer(x, k, v, p["g_mix_pre"], p["w_uq"], p["w_gate"], p["bias"], p["w_pool"],
               p["pool_scale"], p["w_out"], p["g_mix_post"])
    return _ffn(x, p["g_ffn_pre"], p["w_gu"], p["w_down"], p["g_ffn_post"])


def kernel(x_prompt, x_sample, norm_mix_pre, w_in, w_pool_grp, pool_scale, attn_rpb, w_out,
           norm_mix_post, norm_ffn_pre, w_gate_up, w_down, norm_ffn_post):
    depth = w_in.shape[0]
    bf16 = jnp.bfloat16
    layers = []
    for l in range(depth):
        w_in_l = w_in[l].astype(bf16)
        layers.append(dict(
            g_mix_pre=norm_mix_pre[l][None, :],
            w_uq=w_in_l[:, :O_K],
            wk_t=w_in_l[:, O_K:O_V].T,
            w_v=w_in_l[:, O_V:O_GP],
            w_gate=w_in_l[:, O_GP:],
            w_pool=w_pool_grp[l].astype(bf16),
            pool_scale=pool_scale[l][None, :],
            bias=_bias_table(attn_rpb[l]),
            w_out=w_out[l].astype(bf16),
            g_mix_post=norm_mix_post[l][None, :],
            g_ffn_pre=norm_ffn_pre[l][None, :],
            w_gu=w_gate_up[l].astype(bf16),
            w_down=w_down[l].astype(bf16),
            g_ffn_post=norm_ffn_post[l][None, :],
        ))

    def trunk(x):
        for p in layers:
            x = _layer(x, p)
        return x

    return (trunk(x_prompt), trunk(x_sample))
```
